```python
import math
import jax, jax.numpy as jnp
from jax import lax
import numpy as np

D_MODEL = 1024
BATCH = 8
SEQ = 4096
DEPTH = 4

CHUNK = 64
QBLOCK = 128
SB_HEAD_DIM = 64
SB_HEADS = D_MODEL // SB_HEAD_DIM
RET_HEADS = 4
RET_QK_DIM = D_MODEL // RET_HEADS
RET_V_DIM = 2 * RET_QK_DIM
RET_DECAY_BASE = 5.0
ROPE_BASE = 10000.0
D_FF = ((8 * D_MODEL // 3 + 127) // 128) * 128
CONV_WIDTH = 3
NORM_EPS = 1e-6
GN_EPS = 1e-5

N_SB_LAYERS = (DEPTH + 1) // 2
N_RET_LAYERS = DEPTH // 2

kernel_name = "hybrid_stickbreak_retention_convffn"


def rms_norm(x, g):
    xf = x.astype(jnp.float32)
    y = xf * lax.rsqrt(jnp.mean(xf * xf, axis=-1, keepdims=True) + NORM_EPS)
    return (y * g.astype(jnp.float32)).astype(x.dtype)


def rotary(x, pos):
    half = x.shape[-1] // 2
    inv_freq = ROPE_BASE ** (-jnp.arange(half, dtype=jnp.float32) / half)
    ang = pos[:, None] * inv_freq[None, :]
    cos = jnp.cos(ang)[None, :, None, :]
    sin = jnp.sin(ang)[None, :, None, :]
    x1, x2 = x[..., :half], x[..., half:]
    return jnp.concatenate([x1 * cos - x2 * sin, x1 * sin + x2 * cos], axis=-1)


def stick_breaking_attention(h, w_qkv, w_o):
    B, S, _ = h.shape
    qkv = (h @ w_qkv).astype(jnp.float32).reshape(B, S, 3, SB_HEADS, SB_HEAD_DIM)
    q = qkv[:, :, 0].transpose(0, 2, 1, 3)
    k = qkv[:, :, 1].transpose(0, 2, 1, 3)
    v = qkv[:, :, 2].transpose(0, 2, 1, 3)
    scale = 1.0 / math.sqrt(SB_HEAD_DIM)
    outs = []
    for blk in range(S // QBLOCK):
        start = blk * QBLOCK
        end = start + QBLOCK
        z = jnp.einsum('bhqd,bhkd->bhqk', q[:, :, start:end], k[:, :, :end]) * scale
        t_idx = start + jnp.arange(QBLOCK)
        s_idx = jnp.arange(end)
        mask = s_idx[None, :] < t_idx[:, None]
        log_one_minus = jnp.where(mask, -jax.nn.softplus(z), 0.0)
        between = lax.cumsum(log_one_minus, axis=3, reverse=True) - log_one_minus
        att = jnp.where(mask, jnp.exp(jax.nn.log_sigmoid(z) + between), 0.0)
        outs.append(jnp.einsum('bhqk,bhkd->bhqd', att, v[:, :, :end]))
    o = jnp.concatenate(outs, axis=2)
    o = o.transpose(0, 2, 1, 3).reshape(B, S, SB_HEADS * SB_HEAD_DIM)
    return o.astype(h.dtype) @ w_o


def retention(h, w_in, gn_g, w_o):
    B, S, _ = h.shape
    nc = S // CHUNK
    proj = (h @ w_in).astype(jnp.float32)
    dq = RET_HEADS * RET_QK_DIM
    dv = RET_HEADS * RET_V_DIM
    q = proj[..., :dq].reshape(B, S, RET_HEADS, RET_QK_DIM)
    k = proj[..., dq:2 * dq].reshape(B, S, RET_HEADS, RET_QK_DIM)
    v = proj[..., 2 * dq:2 * dq + dv].reshape(B, S, RET_HEADS, RET_V_DIM)
    g = proj[..., 2 * dq + dv:]
    pos = jnp.arange(S, dtype=jnp.float32)
    q = rotary(q, pos)
    k = rotary(k, pos) * (RET_QK_DIM ** -0.5)

    log_gamma = jnp.log1p(-jnp.exp2(-RET_DECAY_BASE - jnp.arange(RET_HEADS, dtype=jnp.float32)))
    i = jnp.arange(CHUNK, dtype=jnp.float32)
    d_intra = jnp.exp(log_gamma[:, None, None] * jnp.abs(i[:, None] - i[None, :]))
    q_dec = jnp.exp(log_gamma[:, None] * (i + 1.0))[None, :, :, None]
    k_dec = jnp.exp(log_gamma[:, None] * (CHUNK - 1.0 - i))[None, :, :, None]
    s_dec = jnp.exp(log_gamma * CHUNK)[None, :, None, None]

    def to_chunks(a):
        return a.reshape(B, nc, CHUNK, RET_HEADS, a.shape[-1]).transpose(1, 0, 3, 2, 4)

    qc, kc, vc = to_chunks(q), to_chunks(k), to_chunks(v)

    def step(state, inp):
        qb, kb, vb = inp
        inter = jnp.einsum('bhid,bhde->bhie', qb * q_dec, state)
        scores = jnp.einsum('bhid,bhjd->bhij', qb, kb) * d_intra[None]
        intra = jnp.einsum('bhij,bhje->bhie', scores, vb)
        new_state = s_dec * state + jnp.einsum('bhjd,bhje->bhde', kb * k_dec, vb)
        return new_state, inter + intra

    state0 = jnp.zeros((B, RET_HEADS, RET_QK_DIM, RET_V_DIM), jnp.float32)
    _, ys = lax.scan(step, state0, (qc, kc, vc))
    o = ys.transpose(1, 0, 3, 2, 4).reshape(B, S, RET_HEADS, RET_V_DIM)
    mu = jnp.mean(o, axis=-1, keepdims=True)
    var = jnp.mean(jnp.square(o - mu), axis=-1, keepdims=True)
    o = ((o - mu) * lax.rsqrt(var + GN_EPS)).reshape(B, S, dv) * gn_g.astype(jnp.float32)
    o = jax.nn.silu(g) * o
    return o.astype(h.dtype) @ w_o


def conv_ffn(h, w_up, conv_w, conv_b, w_down):
    u = h @ w_up
    u = lax.conv_general_dilated(
        u, conv_w[:, None, :].astype(u.dtype), window_strides=(1,),
        padding=[(CONV_WIDTH - 1, 0)], dimension_numbers=('NWC', 'WIO', 'NWC'),
        feature_group_count=u.shape[-1]) + conv_b
    gate, val = u[..., :D_FF], u[..., D_FF:]
    return (jax.nn.gelu(gate) * val) @ w_down


def setup_inputs(seed: int = 0) -> dict:
    key = jax.random.key(seed)
    ks = jax.random.split(key, 16)
    D = D_MODEL
    f32 = jnp.float32

    def nrm(k, shape, fan_in):
        return jax.random.normal(k, shape, f32) * (fan_in ** -0.5)

    def gain(k, shape):
        return 1.0 + 0.02 * jax.random.normal(k, shape, f32)

    ret_in_cols = 2 * RET_HEADS * RET_QK_DIM + 2 * RET_HEADS * RET_V_DIM
    return {
        "x": jax.random.normal(ks[0], (BATCH, SEQ, D), f32),
        "norm_mix_pre": gain(ks[1], (DEPTH, D)),
        "norm_mix_post": gain(ks[2], (DEPTH, D)),
        "sb_w_qkv": nrm(ks[3], (N_SB_LAYERS, D, 3 * SB_HEADS * SB_HEAD_DIM), D),
        "sb_w_o": nrm(ks[4], (N_SB_LAYERS, SB_HEADS * SB_HEAD_DIM, D), SB_HEADS * SB_HEAD_DIM),
        "ret_w_in": nrm(ks[5], (N_RET_LAYERS, D, ret_in_cols), D),
        "ret_gn": gain(ks[6], (N_RET_LAYERS, RET_HEADS * RET_V_DIM)),
        "ret_w_o": nrm(ks[7], (N_RET_LAYERS, RET_HEADS * RET_V_DIM, D), RET_HEADS * RET_V_DIM),
        "norm_ffn_pre": gain(ks[8], (DEPTH, D)),
        "norm_ffn_post": gain(ks[9], (DEPTH, D)),
        "ffn_w_up": nrm(ks[10], (DEPTH, D, 2 * D_FF), D),
        "ffn_conv_w": nrm(ks[11], (DEPTH, CONV_WIDTH, 2 * D_FF), CONV_WIDTH),
        "ffn_conv_b": 0.01 * jax.random.normal(ks[12], (DEPTH, 2 * D_FF), f32),
        "ffn_w_down": nrm(ks[13], (DEPTH, D_FF, D), D_FF),
    }


def reference(x, norm_mix_pre, norm_mix_post, sb_w_qkv, sb_w_o, ret_w_in, ret_gn, ret_w_o,
              norm_ffn_pre, norm_ffn_post, ffn_w_up, ffn_conv_w, ffn_conv_b, ffn_w_down):
    h = x
    for layer in range(DEPTH):
        hn = rms_norm(h, norm_mix_pre[layer])
        j = layer // 2
        if layer % 2 == 0:
            mix = stick_breaking_attention(hn, sb_w_qkv[j], sb_w_o[j])
        else:
            mix = retention(hn, ret_w_in[j], ret_gn[j], ret_w_o[j])
        h = h + rms_norm(mix, norm_mix_post[layer])
        hn = rms_norm(h, norm_ffn_pre[layer])
        ff = conv_ffn(hn, ffn_w_up[layer], ffn_conv_w[layer], ffn_conv_b[layer], ffn_w_down[layer])
        h = h + rms_norm(ff, norm_ffn_post[layer])
    return h
```

```python
import functools
import math

import jax
import jax.numpy as jnp
from jax import lax
from jax.experimental import pallas as pl
from jax.experimental.pallas import tpu as pltpu

F32 = jnp.float32
BF16 = jnp.bfloat16

NORM_EPS = 1e-6
GN_EPS = 1e-5
ROPE_BASE = 10000.0
RET_DECAY_BASE = 5.0
CHUNK = 64
SB_HEAD_DIM = 64
RET_HEADS = 4
CONV_WIDTH = 3

LANES = 128
SB_TILE = 128
RET_BLOCK = 256
SB_LOG_ZERO = -104.0

VMEM_LIMIT = 48 * 1024 * 1024


def _cparams(sem):
    return pltpu.CompilerParams(dimension_semantics=sem, vmem_limit_bytes=VMEM_LIMIT)


def _rms(x, g):
    ms = jnp.mean(x * x, axis=-1, keepdims=True)
    return x * lax.rsqrt(ms + NORM_EPS) * g


def _norm_proj_kernel(x_ref, g_ref, w_ref, *rest, mode, tn):
    if mode == "ret":
        cos_ref, sin_ref, o_ref, xn_ref = rest
    else:
        o_ref, xn_ref = rest
    j = pl.program_id(1)

    @pl.when(j == 0)
    def _():
        xn_ref[...] = _rms(x_ref[...], g_ref[...]).astype(BF16)

    acc = jnp.dot(xn_ref[...], w_ref[...], preferred_element_type=F32)

    if mode == "sb":
        n_q = 1024 // tn
        scale = jnp.where(j < n_q, 1.0 / math.sqrt(SB_HEAD_DIM), 1.0).astype(F32)
        o_ref[...] = (acc * scale).astype(o_ref.dtype)
    elif mode == "ret":
        n_q = 1024 // tn
        n_qk = 2048 // tn

        @pl.when(j < n_qk)
        def _():
            cos = cos_ref[...]
            sin = sin_ref[...]
            scale = jnp.where(j < n_q, 1.0, 1.0 / 16.0).astype(F32)
            for hh in range(tn // 256):
                x1 = acc[:, hh * 256:hh * 256 + 128]
                x2 = acc[:, hh * 256 + 128:hh * 256 + 256]
                o_ref[:, hh * 256:hh * 256 + 128] = ((x1 * cos - x2 * sin) * scale).astype(o_ref.dtype)
                o_ref[:, hh * 256 + 128:hh * 256 + 256] = ((x1 * sin + x2 * cos) * scale).astype(o_ref.dtype)

        @pl.when(j >= n_qk)
        def _():
            o_ref[...] = acc.astype(o_ref.dtype)
    else:
        o_ref[...] = acc.astype(o_ref.dtype)


def _norm_proj(x2d, g, w, mode, rope=None, tm=1024, tn=512):
    M, D = x2d.shape
    N = w.shape[1]
    in_specs = [
        pl.BlockSpec((tm, D), lambda i, j: (i, 0)),
        pl.BlockSpec((1, D), lambda i, j: (0, 0)),
        pl.BlockSpec((D, tn), lambda i, j: (0, j)),
    ]
    args = [x2d, g.reshape(1, D), w]
    if mode == "ret":
        cos, sin = rope
        S = cos.shape[0]
        nb = S // tm
        in_specs += [pl.BlockSpec((tm, 128), lambda i, j: (i % nb, 0)),
                     pl.BlockSpec((tm, 128), lambda i, j: (i % nb, 0))]
        args += [cos, sin]
    return pl.pallas_call(
        functools.partial(_norm_proj_kernel, mode=mode, tn=tn),
        grid=(M // tm, N // tn),
        in_specs=in_specs,
        out_specs=pl.BlockSpec((tm, tn), lambda i, j: (i, j)),
        out_shape=jax.ShapeDtypeStruct((M, N), BF16),
        scratch_shapes=[pltpu.VMEM((tm, D), BF16)],
        compiler_params=_cparams(("parallel", "arbitrary")),
        name="norm_proj_" + mode,
    )(*args)


def _sb_attn_kernel(q_ref, k_ref, v_ref, o_ref, *, nsub):
    T = SB_TILE
    dh = SB_HEAD_DIM
    qt = pl.program_id(2)
    row = lax.broadcasted_iota(jnp.int32, (T, T), 0)
    col = lax.broadcasted_iota(jnp.int32, (T, T), 1)
    upper = jnp.where(row > col, 1.0, 0.0).astype(BF16)
    causal = col < row

    def tile(kb, q2, carry, acc, diag):
        ks = pl.multiple_of(kb * T, T)
        k2 = k_ref[0, pl.ds(ks, T), :]
        v2 = v_ref[0, pl.ds(ks, T), :]
        new_carry, new_acc = [], []
        for h in range(2):
            q = q2[:, h * dh:(h + 1) * dh]
            k = k2[:, h * dh:(h + 1) * dh]
            v = v2[:, h * dh:(h + 1) * dh]
            z = lax.dot_general(q, k, (((1,), (1,)), ((), ())), preferred_element_type=F32)
            sp = jnp.maximum(z, 0.0) + jnp.log1p(jnp.exp(-jnp.abs(z)))
            lom = -sp
            if diag:
                lom = jnp.where(causal, lom, 0.0)
            hi = lom.astype(BF16)
            lo = (lom - hi.astype(F32)).astype(BF16)
            cs = (jnp.dot(hi, upper, preferred_element_type=F32)
                  + jnp.dot(lo, upper, preferred_element_type=F32))
            logp = (z - sp) + cs + carry[h]
            p = jnp.exp(logp)
            if diag:
                p = jnp.where(causal, p, 0.0)
            new_acc.append(acc[h] + jnp.dot(p.astype(BF16), v, preferred_element_type=F32))
            new_carry.append(carry[h] + jnp.sum(lom, axis=1, keepdims=True))
        return tuple(new_carry), tuple(new_acc)

    for sub in range(nsub):
        qb = qt * nsub + sub
        q2 = q_ref[0, sub * T:(sub + 1) * T, :]
        zc = jnp.zeros((T, 1), F32)
        za = jnp.zeros((T, dh), F32)
        carry, acc = tile(qb, q2, (zc, zc), (za, za), True)

        def near(i, st, qb=qb, q2=q2):
            return tile(qb - 1 - i, q2, st[0], st[1], False)

        carry, acc = lax.fori_loop(0, jnp.minimum(qb, 2), near, (carry, acc))

        def live(st):
            kb, carry, _ = st
            return jnp.logical_and(kb >= 0,
                                   jnp.max(jnp.maximum(carry[0], carry[1])) > SB_LOG_ZERO)

        def far(st, q2=q2):
            kb, carry, acc = st
            carry, acc = tile(kb, q2, carry, acc, False)
            return kb - 1, carry, acc

        _, carry, acc = lax.while_loop(live, far, (qb - 3, carry, acc))
        o_ref[0, sub * T:(sub + 1) * T, :] = jnp.concatenate(acc, axis=1).astype(o_ref.dtype)


def _sb_attention(qkv, B, S, tq=512):
    nsub = tq // SB_TILE
    n_pairs = 1024 // LANES
    return pl.pallas_call(
        functools.partial(_sb_attn_kernel, nsub=nsub),
        grid=(B, n_pairs, S // tq),
        in_specs=[
            pl.BlockSpec((1, tq, LANES), lambda b, h, t: (b, t, h)),
            pl.BlockSpec((1, S, LANES), lambda b, h, t: (b, 0, n_pairs + h)),
            pl.BlockSpec((1, S, LANES), lambda b, h, t: (b, 0, 2 * n_pairs + h)),
        ],
        out_specs=pl.BlockSpec((1, tq, LANES), lambda b, h, t: (b, t, h)),
        out_shape=jax.ShapeDtypeStruct((B, S, 1024), BF16),
        compiler_params=_cparams(("parallel", "parallel", "arbitrary")),
        name="sb_attention",
    )(qkv, qkv, qkv)


def _retention_kernel(sdec_ref, q_ref, k_ref, v_ref, g_ref, gn_ref, d_ref, qdec_ref, kdec_ref,
                      o_ref, state_ref):
    h = pl.program_id(1)
    t = pl.program_id(2)

    @pl.when(t == 0)
    def _():
        state_ref[...] = jnp.zeros_like(state_ref)

    q = q_ref[0]
    k = k_ref[0]
    v = v_ref[0]
    state = state_ref[...]
    scores = lax.dot_general(q, k, (((1,), (1,)), ((), ())), preferred_element_type=F32)
    a = (scores * d_ref[0]).astype(BF16)
    intra = jnp.dot(a, v, preferred_element_type=F32)
    qd = (q.astype(F32) * qdec_ref[0]).astype(BF16)
    inter = jnp.dot(qd, state.astype(BF16), preferred_element_type=F32)
    o = inter + intra
    kd = (k.astype(F32) * kdec_ref[0]).astype(BF16)
    upd = lax.dot_general(kd, v, (((0,), (0,)), ((), ())), preferred_element_type=F32)
    state_ref[...] = sdec_ref[h] * state + upd

    mu = jnp.mean(o, axis=-1, keepdims=True)
    oc = o - mu
    var = jnp.mean(oc * oc, axis=-1, keepdims=True)
    on = oc * lax.rsqrt(var + GN_EPS) * gn_ref[...]
    gate = g_ref[0].astype(F32)
    o_ref[0] = (gate * jax.nn.sigmoid(gate) * on).astype(o_ref.dtype)


def _retention(proj, gn, consts, B, S):
    T = RET_BLOCK
    H = RET_HEADS
    sdec, dmat, qdec, kdec = consts
    return pl.pallas_call(
        _retention_kernel,
        grid=(B, H, S // T),
        in_specs=[
            pl.BlockSpec(memory_space=pltpu.SMEM),
            pl.BlockSpec((1, T, 256), lambda b, h, t: (b, t, h)),
            pl.BlockSpec((1, T, 256), lambda b, h, t: (b, t, H + h)),
            pl.BlockSpec((1, T, 512), lambda b, h, t: (b, t, H + h)),
            pl.BlockSpec((1, T, 512), lambda b, h, t: (b, t, 2 * H + h)),
            pl.BlockSpec((1, 512), lambda b, h, t: (0, h)),
            pl.BlockSpec((1, T, T), lambda b, h, t: (h, 0, 0)),
            pl.BlockSpec((1, T, 256), lambda b, h, t: (h, 0, 0)),
            pl.BlockSpec((1, T, 256), lambda b, h, t: (h, 0, 0)),
        ],
        out_specs=pl.BlockSpec((1, T, 512), lambda b, h, t: (b, t, h)),
        out_shape=jax.ShapeDtypeStruct((B, S, H * 512), BF16),
        scratch_shapes=[pltpu.VMEM((256, 512), F32)],
        compiler_params=_cparams(("parallel", "parallel", "arbitrary")),
        name="retention",
    )(sdec, proj, proj, proj, proj, gn.reshape(1, -1), dmat, qdec, kdec)


def _retention_consts():
    T = RET_BLOCK
    log_gamma = jnp.log1p(-jnp.exp2(-RET_DECAY_BASE - jnp.arange(RET_HEADS, dtype=F32)))
    i = jnp.arange(T, dtype=F32)
    dist = jnp.abs(i[:, None] - i[None, :])
    chunk = jnp.arange(T) // CHUNK
    visible = chunk[None, :] <= chunk[:, None]
    dmat = jnp.where(visible[None], jnp.exp(log_gamma[:, None, None] * dist[None]), 0.0)
    qdec = jnp.exp(log_gamma[:, None] * (i[None] + 1.0))
    kdec = jnp.exp(log_gamma[:, None] * (T - 1.0 - i[None]))
    qdec = jnp.broadcast_to(qdec[:, :, None], (RET_HEADS, T, 256))
    kdec = jnp.broadcast_to(kdec[:, :, None], (RET_HEADS, T, 256))
    sdec = jnp.exp(log_gamma * T)
    return sdec.astype(F32), dmat.astype(F32), qdec.astype(F32), kdec.astype(F32)


def _rope_tables(S):
    half = 128
    inv_freq = ROPE_BASE ** (-jnp.arange(half, dtype=F32) / half)
    ang = jnp.arange(S, dtype=F32)[:, None] * inv_freq[None, :]
    return jnp.cos(ang), jnp.sin(ang)


def _out_proj_kernel(o_ref, w_ref, g_ref, h_ref, y_ref):
    mix = jnp.dot(o_ref[...], w_ref[...], preferred_element_type=F32)
    y_ref[...] = h_ref[...] + _rms(mix, g_ref[...])


def _out_proj(o2d, w, g, h2d, tm=512):
    M, K = o2d.shape
    D = w.shape[1]
    return pl.pallas_call(
        _out_proj_kernel,
        grid=(M // tm,),
        in_specs=[
            pl.BlockSpec((tm, K), lambda i: (i, 0)),
            pl.BlockSpec((K, D), lambda i: (0, 0)),
            pl.BlockSpec((1, D), lambda i: (0, 0)),
            pl.BlockSpec((tm, D), lambda i: (i, 0)),
        ],
        out_specs=pl.BlockSpec((tm, D), lambda i: (i, 0)),
        out_shape=jax.ShapeDtypeStruct((M, D), F32),
        compiler_params=_cparams(("parallel",)),
        name="out_proj",
    )(o2d, w, g.reshape(1, D), h2d)


def _ffn_kernel(h_ref, gpre_ref, wup_ref, cw_ref, cb_ref, wdn_ref, gpost_ref, y_ref,
                xn_ref, acc_ref, halo_ref, *, tm, cw, d_ff, tiles_per_seq):
    i = pl.program_id(0)

    @pl.when(i % tiles_per_seq == 0)
    def _():
        halo_ref[...] = jnp.zeros_like(halo_ref)

    xn_ref[...] = _rms(h_ref[...], gpre_ref[...]).astype(BF16)
    acc_ref[...] = jnp.zeros_like(acc_ref)

    def conv_half(off):
        u = jnp.dot(xn_ref[...], wup_ref[:, pl.ds(off, cw)], preferred_element_type=F32)
        prev = halo_ref[:, pl.ds(off, cw)]
        halo_ref[:, pl.ds(off, cw)] = u[tm - 8:, :]
        ext = jnp.concatenate([prev, u], axis=0)
        w = cw_ref[:, pl.ds(off, cw)]
        return (w[2:3] * u + w[1:2] * ext[7:tm + 7] + w[0:1] * ext[6:tm + 6]
                + cb_ref[:, pl.ds(off, cw)])

    def chunk(c, _):
        off = pl.multiple_of(c * cw, cw)
        gate = conv_half(off)
        val = conv_half(pl.multiple_of(d_ff + off, LANES))
        act = (jax.nn.gelu(gate) * val).astype(BF16)
        acc_ref[...] += jnp.dot(act, wdn_ref[pl.ds(off, cw), :], preferred_element_type=F32)
        return 0

    lax.fori_loop(0, d_ff // cw, chunk, 0)
    y_ref[...] = h_ref[...] + _rms(acc_ref[...], gpost_ref[...])


def _conv_ffn(h2d, gpre, wup, conv_w, conv_b, wdn, gpost, S, tm=512, cw=256):
    M, D = h2d.shape
    d_ff = wdn.shape[0]
    const = dict(pipeline_mode=pl.Buffered(1))
    return pl.pallas_call(
        functools.partial(_ffn_kernel, tm=tm, cw=cw, d_ff=d_ff, tiles_per_seq=S // tm),
        grid=(M // tm,),
        in_specs=[
            pl.BlockSpec((tm, D), lambda i: (i, 0)),
            pl.BlockSpec((1, D), lambda i: (0, 0)),
            pl.BlockSpec((D, 2 * d_ff), lambda i: (0, 0), **const),
            pl.BlockSpec((CONV_WIDTH, 2 * d_ff), lambda i: (0, 0)),
            pl.BlockSpec((1, 2 * d_ff), lambda i: (0, 0)),
            pl.BlockSpec((d_ff, D), lambda i: (0, 0), **const),
            pl.BlockSpec((1, D), lambda i: (0, 0)),
        ],
        out_specs=pl.BlockSpec((tm, D), lambda i: (i, 0)),
        out_shape=jax.ShapeDtypeStruct((M, D), F32),
        scratch_shapes=[
            pltpu.VMEM((tm, D), BF16),
            pltpu.VMEM((tm, D), F32),
            pltpu.VMEM((8, 2 * d_ff), F32),
        ],
        compiler_params=_cparams(("arbitrary",)),
        name="conv_ffn",
    )(h2d, gpre.reshape(1, D), wup, conv_w, conv_b.reshape(1, -1), wdn, gpost.reshape(1, D))


def kernel(x, norm_mix_pre, norm_mix_post, sb_w_qkv, sb_w_o, ret_w_in, ret_gn, ret_w_o,
           norm_ffn_pre, norm_ffn_post, ffn_w_up, ffn_conv_w, ffn_conv_b, ffn_w_down):
    B, S, D = x.shape
    depth = norm_mix_pre.shape[0]
    h = x.reshape(B * S, D)
    rope = _rope_tables(S)
    ret_consts = _retention_consts()
    for layer in range(depth):
        j = layer // 2
        if layer % 2 == 0:
            qkv = _norm_proj(h, norm_mix_pre[layer], sb_w_qkv[j].astype(BF16), "sb")
            o = _sb_attention(qkv.reshape(B, S, -1), B, S)
            w_o = sb_w_o[j]
        else:
            proj = _norm_proj(h, norm_mix_pre[layer], ret_w_in[j].astype(BF16), "ret", rope=rope)
            o = _retention(proj.reshape(B, S, -1), ret_gn[j], ret_consts, B, S)
            w_o = ret_w_o[j]
        h = _out_proj(o.reshape(B * S, -1), w_o.astype(BF16), norm_mix_post[layer], h)
        h = _conv_ffn(h, norm_ffn_pre[layer], ffn_w_up[layer].astype(BF16), ffn_conv_w[layer],
                      ffn_conv_b[layer], ffn_w_down[layer].astype(BF16), norm_ffn_post[layer], S)
    return h.reshape(B, S, D)
```

```python
import functools
import math

import jax
import jax.numpy as jnp
from jax import lax
from jax.experimental import pallas as pl
from jax.experimental.pallas import tpu as pltpu

F32 = jnp.float32
BF16 = jnp.bfloat16

NORM_EPS = 1e-6
GN_EPS = 1e-5
ROPE_BASE = 10000.0
RET_DECAY_BASE = 5.0
CHUNK = 64
SB_HEAD_DIM = 64
RET_HEADS = 4
CONV_WIDTH = 3

LANES = 128
SB_TILE = 128
RET_BLOCK = 256
SB_LOG_ZERO = -104.0

VMEM_LIMIT = 48 * 1024 * 1024


def _cparams(sem):
    return pltpu.CompilerParams(dimension_semantics=sem, vmem_limit_bytes=VMEM_LIMIT)


def _rms(x, g):
    ms = jnp.mean(x * x, axis=-1, keepdims=True)
    return x * lax.rsqrt(ms + NORM_EPS) * g


def _norm_proj_kernel(x_ref, g_ref, w_ref, *rest, mode, tn):
    if mode == "ret":
        cos_ref, sin_ref, o_ref, xn_ref = rest
    else:
        o_ref, xn_ref = rest
    j = pl.program_id(1)

    @pl.when(j == 0)
    def _():
        xn_ref[...] = _rms(x_ref[...], g_ref[...]).astype(BF16)

    acc = jnp.dot(xn_ref[...], w_ref[...], preferred_element_type=F32)

    if mode == "sb":
        n_q = 1024 // tn
        scale = jnp.where(j < n_q, -1.0 / math.sqrt(SB_HEAD_DIM), 1.0).astype(F32)
        o_ref[...] = (acc * scale).astype(o_ref.dtype)
    elif mode == "ret":
        n_q = 1024 // tn
        n_qk = 2048 // tn

        @pl.when(j < n_qk)
        def _():
            cos = cos_ref[...]
            sin = sin_ref[...]
            scale = jnp.where(j < n_q, 1.0, 1.0 / 16.0).astype(F32)
            for hh in range(tn // 256):
                x1 = acc[:, hh * 256:hh * 256 + 128]
                x2 = acc[:, hh * 256 + 128:hh * 256 + 256]
                o_ref[:, hh * 256:hh * 256 + 128] = ((x1 * cos - x2 * sin) * scale).astype(o_ref.dtype)
                o_ref[:, hh * 256 + 128:hh * 256 + 256] = ((x1 * sin + x2 * cos) * scale).astype(o_ref.dtype)

        @pl.when(j >= n_qk)
        def _():
            o_ref[...] = acc.astype(o_ref.dtype)
    else:
        o_ref[...] = acc.astype(o_ref.dtype)


def _norm_proj(x2d, g, w, mode, rope=None, tm=1024, tn=512):
    M, D = x2d.shape
    N = w.shape[1]
    in_specs = [
        pl.BlockSpec((tm, D), lambda i, j: (i, 0)),
        pl.BlockSpec((1, D), lambda i, j: (0, 0)),
        pl.BlockSpec((D, tn), lambda i, j: (0, j)),
    ]
    args = [x2d, g.reshape(1, D), w]
    if mode == "ret":
        cos, sin = rope
        S = cos.shape[0]
        nb = S // tm
        in_specs += [pl.BlockSpec((tm, 128), lambda i, j: (i % nb, 0)),
                     pl.BlockSpec((tm, 128), lambda i, j: (i % nb, 0))]
        args += [cos, sin]
    return pl.pallas_call(
        functools.partial(_norm_proj_kernel, mode=mode, tn=tn),
        grid=(M // tm, N // tn),
        in_specs=in_specs,
        out_specs=pl.BlockSpec((tm, tn), lambda i, j: (i, j)),
        out_shape=jax.ShapeDtypeStruct((M, N), BF16),
        scratch_shapes=[pltpu.VMEM((tm, D), BF16)],
        compiler_params=_cparams(("parallel", "arbitrary")),
        name="norm_proj_" + mode,
    )(*args)


def _sb_attn_kernel(q_ref, k_ref, v_ref, cum_ref, o_ref, acc_ref, car_ref, *, nsub):
    T = SB_TILE
    qt = pl.program_id(2)
    row = lax.broadcasted_iota(jnp.int32, (T, T), 0)
    col = lax.broadcasted_iota(jnp.int32, (T, T), 1)
    causal = col < row
    head0 = col < SB_HEAD_DIM
    cum = cum_ref[...]
    nt = (((1,), (1,)), ((), ()))

    def head_queries(sub):
        q2 = q_ref[0, sub * T:(sub + 1) * T, :]
        zero = jnp.zeros_like(q2)
        return jnp.where(head0, q2, zero), jnp.where(head0, zero, q2)

    def piece(w, diag):
        lom = jnp.minimum(w, 0.0) - jnp.log(1.0 + jnp.exp(-jnp.abs(w)))
        lsg = lom - w
        if diag:
            lom = jnp.where(causal, lom, 0.0)
        hi = lom.astype(BF16)
        lo = (lom - hi.astype(F32)).astype(BF16)
        return lsg, jnp.concatenate([hi, lo], axis=1)

    def window(qb, n):
        start = (qb - (n - 1)) * T
        return start if isinstance(start, int) else pl.multiple_of(start, T)

    def s_scores(c):
        sub, h, qb, n = c
        k_win = k_ref[0, pl.ds(window(qb, n), n * T), :]
        return lax.dot_general(head_queries(sub)[h], k_win, nt, preferred_element_type=F32)

    def s_pieces(c, w):
        n = c[3]
        parts = [piece(w[:, i * T:(i + 1) * T], i == n - 1) for i in range(n)]
        return [lsg for lsg, _ in parts], jnp.concatenate([hl for _, hl in parts], axis=0)

    def s_cumsum(c, hl):
        return jnp.dot(hl, cum, preferred_element_type=F32)

    def s_weights(c, lsgs, r):
        sub, h, _, n = c
        carry = None
        ps = [None] * n
        for i in reversed(range(n)):
            e = lsgs[i] + r[i * T:(i + 1) * T, :T]
            if carry is not None:
                e = e + carry
            p = jnp.exp(e)
            if i == n - 1:
                p = jnp.where(causal, p, 0.0)
            ps[i] = p.astype(BF16)
            tot = r[i * T:(i + 1) * T, T:]
            carry = tot if carry is None else carry + tot
        car_ref[sub, h] = carry
        return jnp.concatenate(ps, axis=1)

    def s_values(c, p):
        sub, h, qb, n = c
        v_win = v_ref[0, pl.ds(window(qb, n), n * T), :]
        acc_ref[sub, h] = jnp.dot(p, v_win, preferred_element_type=F32)

    def near_windows(chains):
        nc = len(chains)
        w, pc, r, p = {}, {}, {}, {}
        for t in range(nc + 4):
            if t - 4 >= 0:
                s_values(chains[t - 4], p.pop(t - 4))
            if 0 <= t - 3 < nc:
                p[t - 3] = s_weights(chains[t - 3], pc.pop(t - 3)[0], r.pop(t - 3))
            if 0 <= t - 2 < nc:
                r[t - 2] = s_cumsum(chains[t - 2], pc[t - 2][1])
            if 0 <= t - 1 < nc:
                pc[t - 1] = s_pieces(chains[t - 1], w.pop(t - 1))
            if t < nc:
                w[t] = s_scores(chains[t])

    @pl.when(qt == 0)
    def _():
        near_windows([(sub, h, sub, min(sub + 1, 3)) for sub in range(nsub) for h in range(2)])

    @pl.when(qt > 0)
    def _():
        near_windows([(sub, h, qt * nsub + sub, 3) for sub in range(nsub) for h in range(2)])

    def live(st):
        kb, top = st
        return jnp.logical_and(kb >= 0, top > SB_LOG_ZERO)

    tops = [jnp.max(car_ref[sub]) for sub in range(nsub)]
    for sub in range(nsub):
        def far(st, sub=sub):
            kb = st[0]
            ks = pl.multiple_of(kb * T, T)
            k2 = k_ref[0, pl.ds(ks, T), :]
            v2 = v_ref[0, pl.ds(ks, T), :]
            for h, qh in enumerate(head_queries(sub)):
                w = lax.dot_general(qh, k2, nt, preferred_element_type=F32)
                lsg, hl = piece(w, False)
                r = jnp.dot(hl, cum, preferred_element_type=F32)
                p = jnp.exp(lsg + r[:, :T] + car_ref[sub, h])
                acc_ref[sub, h] += jnp.dot(p.astype(BF16), v2, preferred_element_type=F32)
                car_ref[sub, h] += r[:, T:]
            return kb - 1, jnp.max(car_ref[sub])

        lax.while_loop(live, far, (qt * nsub + sub - 3, tops[sub]))

    for sub in range(nsub):
        o_ref[0, sub * T:(sub + 1) * T, :] = jnp.where(
            head0, acc_ref[sub, 0], acc_ref[sub, 1]).astype(o_ref.dtype)


def _sb_attention(qkv, B, S, tq=512):
    nsub = tq // SB_TILE
    n_pairs = 1024 // LANES
    T = SB_TILE
    j = jnp.arange(2 * T)[:, None] % T
    s = jnp.arange(2 * T)[None, :]
    cum = jnp.where((s >= T) | (j > s), 1.0, 0.0).astype(BF16)
    return pl.pallas_call(
        functools.partial(_sb_attn_kernel, nsub=nsub),
        grid=(B, n_pairs, S // tq),
        in_specs=[
            pl.BlockSpec((1, tq, LANES), lambda b, h, t: (b, t, h)),
            pl.BlockSpec((1, S, LANES), lambda b, h, t: (b, 0, n_pairs + h)),
            pl.BlockSpec((1, S, LANES), lambda b, h, t: (b, 0, 2 * n_pairs + h)),
            pl.BlockSpec((2 * T, 2 * T), lambda b, h, t: (0, 0)),
        ],
        out_specs=pl.BlockSpec((1, tq, LANES), lambda b, h, t: (b, t, h)),
        out_shape=jax.ShapeDtypeStruct((B, S, 1024), BF16),
        scratch_shapes=[
            pltpu.VMEM((nsub, 2, SB_TILE, LANES), F32),
            pltpu.VMEM((nsub, 2, SB_TILE, SB_TILE), F32),
        ],
        compiler_params=_cparams(("parallel", "parallel", "arbitrary")),
        name="sb_attention",
    )(qkv, qkv, qkv, cum)


def _retention_kernel(sdec_ref, q_ref, k_ref, v_ref, g_ref, gn_ref, d_ref, qdec_ref, kdec_ref,
                      o_ref, state_ref):
    h = pl.program_id(1)
    t = pl.program_id(2)

    @pl.when(t == 0)
    def _():
        state_ref[...] = jnp.zeros_like(state_ref)

    q = q_ref[0]
    k = k_ref[0]
    v = v_ref[0]
    state = state_ref[...]
    scores = lax.dot_general(q, k, (((1,), (1,)), ((), ())), preferred_element_type=F32)
    a = (scores * d_ref[0]).astype(BF16)
    intra = jnp.dot(a, v, preferred_element_type=F32)
    qd = (q.astype(F32) * qdec_ref[0]).astype(BF16)
    inter = jnp.dot(qd, state.astype(BF16), preferred_element_type=F32)
    o = inter + intra
    kd = (k.astype(F32) * kdec_ref[0]).astype(BF16)
    upd = lax.dot_general(kd, v, (((0,), (0,)), ((), ())), preferred_element_type=F32)
    state_ref[...] = sdec_ref[h] * state + upd

    mu = jnp.mean(o, axis=-1, keepdims=True)
    oc = o - mu
    var = jnp.mean(oc * oc, axis=-1, keepdims=True)
    on = oc * lax.rsqrt(var + GN_EPS) * gn_ref[...]
    gate = g_ref[0].astype(F32)
    o_ref[0] = (gate * jax.nn.sigmoid(gate) * on).astype(o_ref.dtype)


def _retention(proj, gn, consts, B, S):
    T = RET_BLOCK
    H = RET_HEADS
    sdec, dmat, qdec, kdec = consts
    return pl.pallas_call(
        _retention_kernel,
        grid=(B, H, S // T),
        in_specs=[
            pl.BlockSpec(memory_space=pltpu.SMEM),
            pl.BlockSpec((1, T, 256), lambda b, h, t: (b, t, h)),
            pl.BlockSpec((1, T, 256), lambda b, h, t: (b, t, H + h)),
            pl.BlockSpec((1, T, 512), lambda b, h, t: (b, t, H + h)),
            pl.BlockSpec((1, T, 512), lambda b, h, t: (b, t, 2 * H + h)),
            pl.BlockSpec((1, 512), lambda b, h, t: (0, h)),
            pl.BlockSpec((1, T, T), lambda b, h, t: (h, 0, 0)),
            pl.BlockSpec((1, T, 256), lambda b, h, t: (h, 0, 0)),
            pl.BlockSpec((1, T, 256), lambda b, h, t: (h, 0, 0)),
        ],
        out_specs=pl.BlockSpec((1, T, 512), lambda b, h, t: (b, t, h)),
        out_shape=jax.ShapeDtypeStruct((B, S, H * 512), BF16),
        scratch_shapes=[pltpu.VMEM((256, 512), F32)],
        compiler_params=_cparams(("parallel", "parallel", "arbitrary")),
        name="retention",
    )(sdec, proj, proj, proj, proj, gn.reshape(1, -1), dmat, qdec, kdec)


def _retention_consts():
    T = RET_BLOCK
    log_gamma = jnp.log1p(-jnp.exp2(-RET_DECAY_BASE - jnp.arange(RET_HEADS, dtype=F32)))
    i = jnp.arange(T, dtype=F32)
    dist = jnp.abs(i[:, None] - i[None, :])
    chunk = jnp.arange(T) // CHUNK
    visible = chunk[None, :] <= chunk[:, None]
    dmat = jnp.where(visible[None], jnp.exp(log_gamma[:, None, None] * dist[None]), 0.0)
    qdec = jnp.exp(log_gamma[:, None] * (i[None] + 1.0))
    kdec = jnp.exp(log_gamma[:, None] * (T - 1.0 - i[None]))
    qdec = jnp.broadcast_to(qdec[:, :, None], (RET_HEADS, T, 256))
    kdec = jnp.broadcast_to(kdec[:, :, None], (RET_HEADS, T, 256))
    sdec = jnp.exp(log_gamma * T)
    return sdec.astype(F32), dmat.astype(F32), qdec.astype(F32), kdec.astype(F32)


def _rope_tables(S):
    half = 128
    inv_freq = ROPE_BASE ** (-jnp.arange(half, dtype=F32) / half)
    ang = jnp.arange(S, dtype=F32)[:, None] * inv_freq[None, :]
    return jnp.cos(ang), jnp.sin(ang)


def _out_proj_kernel(o_ref, w_ref, g_ref, h_ref, y_ref):
    mix = jnp.dot(o_ref[...], w_ref[...], preferred_element_type=F32)
    y_ref[...] = h_ref[...] + _rms(mix, g_ref[...])


def _out_proj(o2d, w, g, h2d, tm=512):
    M, K = o2d.shape
    D = w.shape[1]
    return pl.pallas_call(
        _out_proj_kernel,
        grid=(M // tm,),
        in_specs=[
            pl.BlockSpec((tm, K), lambda i: (i, 0)),
            pl.BlockSpec((K, D), lambda i: (0, 0)),
            pl.BlockSpec((1, D), lambda i: (0, 0)),
            pl.BlockSpec((tm, D), lambda i: (i, 0)),
        ],
        out_specs=pl.BlockSpec((tm, D), lambda i: (i, 0)),
        out_shape=jax.ShapeDtypeStruct((M, D), F32),
        compiler_params=_cparams(("parallel",)),
        name="out_proj",
    )(o2d, w, g.reshape(1, D), h2d)


def _ffn_kernel(h_ref, gpre_ref, wup_ref, cw_ref, cb_ref, wdn_ref, gpost_ref, y_ref,
                xn_ref, acc_ref, halo_ref, u_ref, *, tm, cw, d_ff, tiles_per_seq):
    i = pl.program_id(0)

    @pl.when(i % tiles_per_seq == 0)
    def _():
        halo_ref[...] = jnp.zeros_like(halo_ref)

    xn_ref[...] = _rms(h_ref[...], gpre_ref[...]).astype(BF16)
    acc_ref[...] = jnp.zeros_like(acc_ref)
    n_chunks = d_ff // cw

    def col(c, base=0):
        off = c * cw + base
        return off if isinstance(off, int) else pl.multiple_of(off, LANES)

    def up(c, slot):
        xn = xn_ref[...]
        u_ref[slot, :, :cw] = jnp.dot(xn, wup_ref[:, pl.ds(col(c), cw)], preferred_element_type=F32)
        u_ref[slot, :, cw:] = jnp.dot(xn, wup_ref[:, pl.ds(col(c, d_ff), cw)],
                                      preferred_element_type=F32)

    def conv(u, off):
        prev = halo_ref[:, pl.ds(off, cw)]
        halo_ref[:, pl.ds(off, cw)] = u[tm - 8:, :]
        ext = jnp.concatenate([prev, u], axis=0)
        w = cw_ref[:, pl.ds(off, cw)]
        return (w[2:3] * u + w[1:2] * ext[7:tm + 7] + w[0:1] * ext[6:tm + 6]
                + cb_ref[:, pl.ds(off, cw)])

    def down(c, slot):
        gate = conv(u_ref[slot, :, :cw], col(c))
        val = conv(u_ref[slot, :, cw:], col(c, d_ff))
        act = (jax.nn.gelu(gate) * val).astype(BF16)
        acc_ref[...] += jnp.dot(act, wdn_ref[pl.ds(col(c), cw), :], preferred_element_type=F32)

    up(0, 0)

    def pair(p, _):
        up(2 * p + 1, 1)
        down(2 * p, 0)
        up(2 * p + 2, 0)
        down(2 * p + 1, 1)
        return 0

    assert n_chunks % 2 == 1
    lax.fori_loop(0, n_chunks // 2, pair, 0)
    down(n_chunks - 1, 0)
    y_ref[...] = h_ref[...] + _rms(acc_ref[...], gpost_ref[...])


def _conv_ffn(h2d, gpre, wup, conv_w, conv_b, wdn, gpost, S, tm=512, cw=256):
    M, D = h2d.shape
    d_ff = wdn.shape[0]
    const = dict(pipeline_mode=pl.Buffered(1))
    return pl.pallas_call(
        functools.partial(_ffn_kernel, tm=tm, cw=cw, d_ff=d_ff, tiles_per_seq=S // tm),
        grid=(M // tm,),
        in_specs=[
            pl.BlockSpec((tm, D), lambda i: (i, 0)),
            pl.BlockSpec((1, D), lambda i: (0, 0)),
            pl.BlockSpec((D, 2 * d_ff), lambda i: (0, 0), **const),
            pl.BlockSpec((CONV_WIDTH, 2 * d_ff), lambda i: (0, 0)),
            pl.BlockSpec((1, 2 * d_ff), lambda i: (0, 0)),
            pl.BlockSpec((d_ff, D), lambda i: (0, 0), **const),
            pl.BlockSpec((1, D), lambda i: (0, 0)),
        ],
        out_specs=pl.BlockSpec((tm, D), lambda i: (i, 0)),
        out_shape=jax.ShapeDtypeStruct((M, D), F32),
        scratch_shapes=[
            pltpu.VMEM((tm, D), BF16),
            pltpu.VMEM((tm, D), F32),
            pltpu.VMEM((8, 2 * d_ff), F32),
            pltpu.VMEM((2, tm, 2 * cw), F32),
        ],
        compiler_params=_cparams(("arbitrary",)),
        name="conv_ffn",
    )(h2d, gpre.reshape(1, D), wup, conv_w, conv_b.reshape(1, -1), wdn, gpost.reshape(1, D))


def kernel(x, norm_mix_pre, norm_mix_post, sb_w_qkv, sb_w_o, ret_w_in, ret_gn, ret_w_o,
           norm_ffn_pre, norm_ffn_post, ffn_w_up, ffn_conv_w, ffn_conv_b, ffn_w_down):
    B, S, D = x.shape
    depth = norm_mix_pre.shape[0]
    h = x.reshape(B * S, D)
    rope = _rope_tables(S)
    ret_consts = _retention_consts()
    for layer in range(depth):
        j = layer // 2
        if layer % 2 == 0:
            qkv = _norm_proj(h, norm_mix_pre[layer], sb_w_qkv[j].astype(BF16), "sb")
            o = _sb_attention(qkv.reshape(B, S, -1), B, S)
            w_o = sb_w_o[j]
        else:
            proj = _norm_proj(h, norm_mix_pre[layer], ret_w_in[j].astype(BF16), "ret", rope=rope)
            o = _retention(proj.reshape(B, S, -1), ret_gn[j], ret_consts, B, S)
            w_o = ret_w_o[j]
        h = _out_proj(o.reshape(B * S, -1), w_o.astype(BF16), norm_mix_post[layer], h)
        h = _conv_ffn(h, norm_ffn_pre[layer], ffn_w_up[layer].astype(BF16), ffn_conv_w[layer],
                      ffn_conv_b[layer], ffn_w_down[layer].astype(BF16), norm_ffn_post[layer], S)
    return h.reshape(B, S, D)
```

```python
import functools
import math

import jax
import jax.numpy as jnp
from jax import lax
from jax.experimental import pallas as pl
from jax.experimental.pallas import tpu as pltpu

F32 = jnp.float32
BF16 = jnp.bfloat16

NORM_EPS = 1e-6
GN_EPS = 1e-5
ROPE_BASE = 10000.0
RET_DECAY_BASE = 5.0
CHUNK = 64
SB_HEAD_DIM = 64
RET_HEADS = 4
RET_QK_DIM = 256
RET_V_DIM = 512
SB_Q_COLS = 1024
RET_QK_COLS = RET_HEADS * RET_QK_DIM
CONV_WIDTH = 3

LANES = 128
SB_TILE = 128
RET_BLOCK = 256
SB_LOG_ZERO = -104.0
LOG2E = 1.4426950408889634

VMEM_LIMIT = 48 * 1024 * 1024
TAIL_VMEM_LIMIT = 56 * 1024 * 1024


def _cparams(sem):
    return pltpu.CompilerParams(dimension_semantics=sem, vmem_limit_bytes=VMEM_LIMIT)


def _rms(x, g):
    ms = jnp.mean(x * x, axis=-1, keepdims=True)
    return x * lax.rsqrt(ms + NORM_EPS) * g


def _norm_proj_kernel(x_ref, g_ref, w_ref, *rest, mode, tn):
    if mode == "ret":
        cos_ref, sin_ref, o_ref, xn_ref = rest
    else:
        o_ref, xn_ref = rest
    n = o_ref.shape[1] // tn
    xn_ref[...] = _rms(x_ref[...], g_ref[...]).astype(BF16)

    def project(j):
        return jnp.dot(xn_ref[...], w_ref[:, j * tn:(j + 1) * tn], preferred_element_type=F32)

    def finish(j, acc):
        c0 = j * tn
        if mode == "sb" and c0 < SB_Q_COLS:
            o_ref[:, c0:c0 + tn] = (acc * (-1.0 / math.sqrt(SB_HEAD_DIM))).astype(o_ref.dtype)
        elif mode == "ret" and c0 < 2 * RET_QK_COLS:
            scale = 1.0 if c0 < RET_QK_COLS else RET_QK_DIM ** -0.5
            cos = cos_ref[...] * scale
            sin = sin_ref[...] * scale
            half = RET_QK_DIM // 2
            for hh in range(tn // RET_QK_DIM):
                lo = hh * RET_QK_DIM
                x1 = acc[:, lo:lo + half]
                x2 = acc[:, lo + half:lo + 2 * half]
                o_ref[:, c0 + lo:c0 + lo + half] = (x1 * cos - x2 * sin).astype(o_ref.dtype)
                o_ref[:, c0 + lo + half:c0 + lo + 2 * half] = (x1 * sin + x2 * cos).astype(o_ref.dtype)
        else:
            o_ref[:, c0:c0 + tn] = acc.astype(o_ref.dtype)

    acc = project(0)
    for j in range(n):
        nxt = project(j + 1) if j + 1 < n else None
        finish(j, acc)
        acc = nxt


def _norm_proj(x2d, g, w, mode, rope=None, tm=512, tn=512):
    M, D = x2d.shape
    N = w.shape[1]
    in_specs = [
        pl.BlockSpec((tm, D), lambda i: (i, 0)),
        pl.BlockSpec((1, D), lambda i: (0, 0)),
        pl.BlockSpec((D, N), lambda i: (0, 0), pipeline_mode=pl.Buffered(1)),
    ]
    args = [x2d, g.reshape(1, D), w]
    if mode == "ret":
        cos, sin = rope
        nb = cos.shape[0] // tm
        in_specs += [pl.BlockSpec((tm, RET_QK_DIM // 2), lambda i: (i % nb, 0)),
                     pl.BlockSpec((tm, RET_QK_DIM // 2), lambda i: (i % nb, 0))]
        args += [cos, sin]
    return pl.pallas_call(
        functools.partial(_norm_proj_kernel, mode=mode, tn=tn),
        grid=(M // tm,),
        in_specs=in_specs,
        out_specs=pl.BlockSpec((tm, N), lambda i: (i, 0)),
        out_shape=jax.ShapeDtypeStruct((M, N), BF16),
        scratch_shapes=[pltpu.VMEM((tm, D), BF16)],
        compiler_params=_cparams(("parallel",)),
        name="norm_proj_" + mode,
    )(*args)


def _sb_attn_kernel(q_ref, k_ref, v_ref, cum_ref, o_ref, acc_ref, car_ref, *, nsub):
    T = SB_TILE
    qt = pl.program_id(2)
    row = lax.broadcasted_iota(jnp.int32, (T, T), 0)
    col = lax.broadcasted_iota(jnp.int32, (T, T), 1)
    causal = col < row
    head0 = col < SB_HEAD_DIM
    cum = cum_ref[...]
    nt = (((1,), (1,)), ((), ()))

    def head_queries(sub):
        q2 = q_ref[0, sub * T:(sub + 1) * T, :]
        zero = jnp.zeros_like(q2)
        return jnp.where(head0, q2, zero), jnp.where(head0, zero, q2)

    def piece(w, diag):
        lom = jnp.minimum(w, 0.0) - jnp.log(1.0 + jnp.exp2(jnp.abs(w) * (-LOG2E)))
        lsg = lom - w
        if diag:
            lom = jnp.where(causal, lom, 0.0)
        hi = lom.astype(BF16)
        lo = (lom - hi.astype(F32)).astype(BF16)
        return lsg, jnp.concatenate([hi, lo], axis=1)

    def window(qb, n):
        start = (qb - (n - 1)) * T
        return start if isinstance(start, int) else pl.multiple_of(start, T)

    def s_scores(c):
        sub, h, qb, n = c
        k_win = k_ref[0, pl.ds(window(qb, n), n * T), :]
        return lax.dot_general(head_queries(sub)[h], k_win, nt, preferred_element_type=F32)

    def s_pieces(c, w):
        n = c[3]
        parts = [piece(w[:, i * T:(i + 1) * T], i == n - 1) for i in range(n)]
        return [lsg for lsg, _ in parts], jnp.concatenate([hl for _, hl in parts], axis=0)

    def s_cumsum(c, hl):
        return jnp.dot(hl, cum, preferred_element_type=F32)

    def s_weights(c, lsgs, r):
        sub, h, _, n = c
        carry = None
        ps = [None] * n
        for i in reversed(range(n)):
            e = lsgs[i] + r[i * T:(i + 1) * T, :T]
            if carry is not None:
                e = e + carry
            p = jnp.exp(e)
            if i == n - 1:
                p = jnp.where(causal, p, 0.0)
            ps[i] = p.astype(BF16)
            tot = r[i * T:(i + 1) * T, T:]
            carry = tot if carry is None else carry + tot
        car_ref[sub, h] = carry
        return jnp.concatenate(ps, axis=1)

    def s_values(c, p):
        sub, h, qb, n = c
        v_win = v_ref[0, pl.ds(window(qb, n), n * T), :]
        acc_ref[sub, h] = jnp.dot(p, v_win, preferred_element_type=F32)

    def near_windows(chains):
        nc = len(chains)
        w, pc, r, p = {}, {}, {}, {}
        for t in range(nc + 4):
            if t - 4 >= 0:
                s_values(chains[t - 4], p.pop(t - 4))
            if 0 <= t - 3 < nc:
                p[t - 3] = s_weights(chains[t - 3], pc.pop(t - 3)[0], r.pop(t - 3))
            if 0 <= t - 2 < nc:
                r[t - 2] = s_cumsum(chains[t - 2], pc[t - 2][1])
            if 0 <= t - 1 < nc:
                pc[t - 1] = s_pieces(chains[t - 1], w.pop(t - 1))
            if t < nc:
                w[t] = s_scores(chains[t])

    @pl.when(qt == 0)
    def _():
        near_windows([(sub, h, sub, min(sub + 1, 3)) for sub in range(nsub) for h in range(2)])

    @pl.when(qt > 0)
    def _():
        near_windows([(sub, h, qt * nsub + sub, 3) for sub in range(nsub) for h in range(2)])

    def live(st):
        kb, top = st
        return jnp.logical_and(kb >= 0, top > SB_LOG_ZERO)

    tops = [jnp.max(car_ref[sub]) for sub in range(nsub)]
    for sub in range(nsub):
        def far(st, sub=sub):
            kb = st[0]
            ks = pl.multiple_of(kb * T, T)
            k2 = k_ref[0, pl.ds(ks, T), :]
            v2 = v_ref[0, pl.ds(ks, T), :]
            for h, qh in enumerate(head_queries(sub)):
                w = lax.dot_general(qh, k2, nt, preferred_element_type=F32)
                lsg, hl = piece(w, False)
                r = jnp.dot(hl, cum, preferred_element_type=F32)
                p = jnp.exp(lsg + r[:, :T] + car_ref[sub, h])
                acc_ref[sub, h] += jnp.dot(p.astype(BF16), v2, preferred_element_type=F32)
                car_ref[sub, h] += r[:, T:]
            return kb - 1, jnp.max(car_ref[sub])

        lax.while_loop(live, far, (qt * nsub + sub - 3, tops[sub]))

    for sub in range(nsub):
        o_ref[0, sub * T:(sub + 1) * T, :] = jnp.where(
            head0, acc_ref[sub, 0], acc_ref[sub, 1]).astype(o_ref.dtype)


def _sb_attention(qkv, B, S, tq=512):
    nsub = tq // SB_TILE
    n_pairs = SB_Q_COLS // LANES
    T = SB_TILE
    j = jnp.arange(2 * T)[:, None] % T
    s = jnp.arange(2 * T)[None, :]
    cum = jnp.where((s >= T) | (j > s), 1.0, 0.0).astype(BF16)
    return pl.pallas_call(
        functools.partial(_sb_attn_kernel, nsub=nsub),
        grid=(B, n_pairs, S // tq),
        in_specs=[
            pl.BlockSpec((1, tq, LANES), lambda b, h, t: (b, t, h)),
            pl.BlockSpec((1, S, LANES), lambda b, h, t: (b, 0, n_pairs + h)),
            pl.BlockSpec((1, S, LANES), lambda b, h, t: (b, 0, 2 * n_pairs + h)),
            pl.BlockSpec((2 * T, 2 * T), lambda b, h, t: (0, 0)),
        ],
        out_specs=pl.BlockSpec((1, tq, LANES), lambda b, h, t: (b, t, h)),
        out_shape=jax.ShapeDtypeStruct((B, S, SB_Q_COLS), BF16),
        scratch_shapes=[
            pltpu.VMEM((nsub, 2, SB_TILE, LANES), F32),
            pltpu.VMEM((nsub, 2, SB_TILE, SB_TILE), F32),
        ],
        compiler_params=_cparams(("parallel", "parallel", "arbitrary")),
        name="sb_attention",
    )(qkv, qkv, qkv, cum)


def _retention_kernel(sdec_ref, q_ref, k_ref, v_ref, g_ref, gn_ref, d_ref, qdec_ref, kdec_ref,
                      o_ref, state_ref):
    t = pl.program_id(1)
    dk, dv = RET_QK_DIM, RET_V_DIM

    @pl.when(t == 0)
    def _():
        state_ref[...] = jnp.zeros_like(state_ref)

    def s_scores(h):
        q = q_ref[0, :, h * dk:(h + 1) * dk]
        k = k_ref[0, :, h * dk:(h + 1) * dk]
        return lax.dot_general(q, k, (((1,), (1,)), ((), ())), preferred_element_type=F32)

    def s_operands(h, scores):
        q = q_ref[0, :, h * dk:(h + 1) * dk]
        k = k_ref[0, :, h * dk:(h + 1) * dk]
        a = (scores * d_ref[h]).astype(BF16)
        qd = (q.astype(F32) * qdec_ref[h]).astype(BF16)
        kd = (k.astype(F32) * kdec_ref[h]).astype(BF16)
        return a, qd, kd, state_ref[h].astype(BF16)

    def s_matmuls(h, ops):
        a, qd, kd, st = ops
        v = v_ref[0, :, h * dv:(h + 1) * dv]
        o = (jnp.dot(qd, st, preferred_element_type=F32)
             + jnp.dot(a, v, preferred_element_type=F32))
        upd = lax.dot_general(kd, v, (((0,), (0,)), ((), ())), preferred_element_type=F32)
        return o, upd

    def s_finish(h, res):
        o, upd = res
        state_ref[h] = sdec_ref[h] * state_ref[h] + upd
        mu = jnp.mean(o, axis=-1, keepdims=True)
        oc = o - mu
        var = jnp.mean(oc * oc, axis=-1, keepdims=True)
        on = oc * lax.rsqrt(var + GN_EPS) * gn_ref[:, h * dv:(h + 1) * dv]
        gate = g_ref[0, :, h * dv:(h + 1) * dv].astype(F32)
        o_ref[0, :, h * dv:(h + 1) * dv] = (gate * jax.nn.sigmoid(gate) * on).astype(o_ref.dtype)

    nh = RET_HEADS
    sc, ops, res = {}, {}, {}
    for step in range(nh + 3):
        if 0 <= step - 3 < nh:
            s_finish(step - 3, res.pop(step - 3))
        if 0 <= step - 2 < nh:
            res[step - 2] = s_matmuls(step - 2, ops.pop(step - 2))
        if 0 <= step - 1 < nh:
            ops[step - 1] = s_operands(step - 1, sc.pop(step - 1))
        if step < nh:
            sc[step] = s_scores(step)


def _retention(proj, gn, consts, B, S):
    T = RET_BLOCK
    H = RET_HEADS
    qk, vw = H * RET_QK_DIM, H * RET_V_DIM
    sdec, dmat, qdec, kdec = consts
    whole = lambda b, t: (0, 0, 0)
    return pl.pallas_call(
        _retention_kernel,
        grid=(B, S // T),
        in_specs=[
            pl.BlockSpec(memory_space=pltpu.SMEM),
            pl.BlockSpec((1, T, qk), lambda b, t: (b, t, 0)),
            pl.BlockSpec((1, T, qk), lambda b, t: (b, t, 1)),
            pl.BlockSpec((1, T, vw), lambda b, t: (b, t, 2 * qk // vw)),
            pl.BlockSpec((1, T, vw), lambda b, t: (b, t, 2 * qk // vw + 1)),
            pl.BlockSpec((1, vw), lambda b, t: (0, 0)),
            pl.BlockSpec((H, T, T), whole),
            pl.BlockSpec((H, T, RET_QK_DIM), whole),
            pl.BlockSpec((H, T, RET_QK_DIM), whole),
        ],
        out_specs=pl.BlockSpec((1, T, vw), lambda b, t: (b, t, 0)),
        out_shape=jax.ShapeDtypeStruct((B, S, vw), BF16),
        scratch_shapes=[pltpu.VMEM((H, RET_QK_DIM, RET_V_DIM), F32)],
        compiler_params=_cparams(("parallel", "arbitrary")),
        name="retention",
    )(sdec, proj, proj, proj, proj, gn.reshape(1, -1), dmat, qdec, kdec)


def _retention_consts():
    T = RET_BLOCK
    log_gamma = jnp.log1p(-jnp.exp2(-RET_DECAY_BASE - jnp.arange(RET_HEADS, dtype=F32)))
    i = jnp.arange(T, dtype=F32)
    dist = jnp.abs(i[:, None] - i[None, :])
    chunk = jnp.arange(T) // CHUNK
    visible = chunk[None, :] <= chunk[:, None]
    dmat = jnp.where(visible[None], jnp.exp(log_gamma[:, None, None] * dist[None]), 0.0)
    qdec = jnp.exp(log_gamma[:, None] * (i[None] + 1.0))
    kdec = jnp.exp(log_gamma[:, None] * (T - 1.0 - i[None]))
    qdec = jnp.broadcast_to(qdec[:, :, None], (RET_HEADS, T, RET_QK_DIM))
    kdec = jnp.broadcast_to(kdec[:, :, None], (RET_HEADS, T, RET_QK_DIM))
    sdec = jnp.exp(log_gamma * T)
    return sdec.astype(F32), dmat.astype(F32), qdec.astype(F32), kdec.astype(F32)


def _rope_tables(S):
    half = RET_QK_DIM // 2
    inv_freq = ROPE_BASE ** (-jnp.arange(half, dtype=F32) / half)
    ang = jnp.arange(S, dtype=F32)[:, None] * inv_freq[None, :]
    return jnp.cos(ang), jnp.sin(ang)


def _tail_kernel(o_ref, h_ref, wo_ref, gmix_ref, gpre_ref, wup_ref, cw_ref, cb_ref, wdn_ref,
                 gpost_ref, y_ref, h1_ref, xn_ref, acc_ref, halo_ref, u_ref,
                 *, tm, cw, d_ff, tiles_per_seq):
    s = pl.program_id(0)
    n_chunks = d_ff // cw
    assert n_chunks % 2 == 1 and n_chunks >= 5
    FIRST, ODD, EVEN = 0, 1, 2

    def col(c, base=0):
        off = c * cw + base
        return off if isinstance(off, int) else pl.multiple_of(off, LANES)

    def up(c, uslot):
        xn = xn_ref[...]
        u_ref[uslot, :, :cw] = jnp.dot(xn, wup_ref[:, pl.ds(col(c), cw)], preferred_element_type=F32)
        u_ref[uslot, :, cw:] = jnp.dot(xn, wup_ref[:, pl.ds(col(c, d_ff), cw)],
                                       preferred_element_type=F32)

    def conv(u, off):
        prev = halo_ref[:, pl.ds(off, cw)]
        halo_ref[:, pl.ds(off, cw)] = u[tm - 8:, :]
        ext = jnp.concatenate([prev, u], axis=0)
        w = cw_ref[:, pl.ds(off, cw)]
        return (w[2:3] * u + w[1:2] * ext[7:tm + 7] + w[0:1] * ext[6:tm + 6]
                + cb_ref[:, pl.ds(off, cw)])

    def down(c, uslot):
        gate = conv(u_ref[uslot, :, :cw], col(c))
        val = conv(u_ref[uslot, :, cw:], col(c, d_ff))
        act = (jax.nn.gelu(gate) * val).astype(BF16)
        acc_ref[...] += jnp.dot(act, wdn_ref[pl.ds(col(c), cw), :], preferred_element_type=F32)

    def mix_proj():
        return jnp.dot(o_ref[...], wo_ref[...], preferred_element_type=F32)

    def mix_norm(mix):
        h1 = h_ref[...] + _rms(mix, gmix_ref[...])
        h1_ref[1] = h1
        xn_ref[...] = _rms(h1, gpre_ref[...]).astype(BF16)

    @pl.when(s == 0)
    def _():
        mix_norm(mix_proj())
        up(0, FIRST)

    @pl.when(s > 0)
    def _():
        @pl.when((s - 1) % tiles_per_seq == 0)
        def _():
            halo_ref[...] = jnp.zeros_like(halo_ref)

        h1_ref[0] = h1_ref[1]
        acc_ref[...] = jnp.zeros_like(acc_ref)

        up(1, ODD)
        down(0, FIRST)
        up(2, EVEN)
        down(1, ODD)

        def pair(p, _):
            up(2 * p + 1, ODD)
            down(2 * p, EVEN)
            up(2 * p + 2, EVEN)
            down(2 * p + 1, ODD)
            return 0

        lax.fori_loop(1, (n_chunks - 3) // 2, pair, 0)
        up(n_chunks - 2, ODD)
        down(n_chunks - 3, EVEN)
        mix = mix_proj()
        up(n_chunks - 1, EVEN)
        down(n_chunks - 2, ODD)
        mix_norm(mix)
        down(n_chunks - 1, EVEN)
        up(0, FIRST)
        y_ref[...] = h1_ref[0] + _rms(acc_ref[...], gpost_ref[...])


def _layer_tail(o2d, h2d, wo, gmix, gpre, wup, conv_w, conv_b, wdn, gpost, S, tm=512, cw=256):
    M, D = h2d.shape
    Ko = o2d.shape[1]
    d_ff = wdn.shape[0]
    n_tiles = M // tm
    const = dict(pipeline_mode=pl.Buffered(1))
    ahead = lambda s: (jnp.minimum(s, n_tiles - 1), 0)
    fixed = lambda s: (0, 0)
    return pl.pallas_call(
        functools.partial(_tail_kernel, tm=tm, cw=cw, d_ff=d_ff, tiles_per_seq=S // tm),
        grid=(n_tiles + 1,),
        in_specs=[
            pl.BlockSpec((tm, Ko), ahead),
            pl.BlockSpec((tm, D), ahead),
            pl.BlockSpec((Ko, D), fixed, **const),
            pl.BlockSpec((1, D), fixed),
            pl.BlockSpec((1, D), fixed),
            pl.BlockSpec((D, 2 * d_ff), fixed, **const),
            pl.BlockSpec((CONV_WIDTH, 2 * d_ff), fixed),
            pl.BlockSpec((1, 2 * d_ff), fixed),
            pl.BlockSpec((d_ff, D), fixed, **const),
            pl.BlockSpec((1, D), fixed),
        ],
        out_specs=pl.BlockSpec((tm, D), lambda s: (jnp.maximum(s - 1, 0), 0)),
        out_shape=jax.ShapeDtypeStruct((M, D), F32),
        scratch_shapes=[
            pltpu.VMEM((2, tm, D), F32),
            pltpu.VMEM((tm, D), BF16),
            pltpu.VMEM((tm, D), F32),
            pltpu.VMEM((8, 2 * d_ff), F32),
            pltpu.VMEM((3, tm, 2 * cw), F32),
        ],
        compiler_params=pltpu.CompilerParams(dimension_semantics=("arbitrary",),
                                             vmem_limit_bytes=TAIL_VMEM_LIMIT),
        name="layer_tail",
    )(o2d, h2d, wo, gmix.reshape(1, D), gpre.reshape(1, D), wup, conv_w, conv_b.reshape(1, -1),
      wdn, gpost.reshape(1, D))


def kernel(x, norm_mix_pre, norm_mix_post, sb_w_qkv, sb_w_o, ret_w_in, ret_gn, ret_w_o,
           norm_ffn_pre, norm_ffn_post, ffn_w_up, ffn_conv_w, ffn_conv_b, ffn_w_down):
    B, S, D = x.shape
    depth = norm_mix_pre.shape[0]
    h = x.reshape(B * S, D)
    rope = _rope_tables(S)
    ret_consts = _retention_consts()
    for layer in range(depth):
        j = layer // 2
        if layer % 2 == 0:
            qkv = _norm_proj(h, norm_mix_pre[layer], sb_w_qkv[j].astype(BF16), "sb")
            o = _sb_attention(qkv.reshape(B, S, -1), B, S)
            w_o = sb_w_o[j]
        else:
            proj = _norm_proj(h, norm_mix_pre[layer], ret_w_in[j].astype(BF16), "ret", rope=rope)
            o = _retention(proj.reshape(B, S, -1), ret_gn[j], ret_consts, B, S)
            w_o = ret_w_o[j]
        h = _layer_tail(o.reshape(B * S, -1), h, w_o.astype(BF16), norm_mix_post[layer],
                        norm_ffn_pre[layer], ffn_w_up[layer].astype(BF16), ffn_conv_w[layer],
                        ffn_conv_b[layer], ffn_w_down[layer].astype(BF16), norm_ffn_post[layer], S)
    return h.reshape(B, S, D)
```

```python
import functools
import math

import jax
import jax.numpy as jnp
from jax import lax
from jax.experimental import pallas as pl
from jax.experimental.pallas import tpu as pltpu

F32 = jnp.float32
BF16 = jnp.bfloat16

NORM_EPS = 1e-6
GN_EPS = 1e-5
ROPE_BASE = 10000.0
RET_DECAY_BASE = 5.0
CHUNK = 64
SB_HEAD_DIM = 64
RET_HEADS = 4
RET_QK_DIM = 256
RET_V_DIM = 512
SB_Q_COLS = 1024
RET_QK_COLS = RET_HEADS * RET_QK_DIM
CONV_WIDTH = 3

LANES = 128
SB_TILE = 128
SB_ROWS = 64
SB_BACK = 2 * SB_TILE - SB_ROWS
SB_GROUP = 8
RET_BLOCK = 256
SB_LOG_ZERO = -104.0
LOG2E = 1.4426950408889634

VMEM_LIMIT = 48 * 1024 * 1024
TAIL_VMEM_LIMIT = 56 * 1024 * 1024


def _cparams(sem):
    return pltpu.CompilerParams(dimension_semantics=sem, vmem_limit_bytes=VMEM_LIMIT)


def _rms(x, g):
    ms = jnp.mean(x * x, axis=-1, keepdims=True)
    return x * lax.rsqrt(ms + NORM_EPS) * g


def _norm_proj_kernel(x_ref, g_ref, w_ref, *rest, mode, tn):
    if mode == "ret":
        cos_ref, sin_ref, o_ref, xn_ref = rest
    else:
        o_ref, xn_ref = rest
    n = o_ref.shape[1] // tn
    xn_ref[...] = _rms(x_ref[...], g_ref[...]).astype(BF16)

    def project(j):
        return jnp.dot(xn_ref[...], w_ref[:, j * tn:(j + 1) * tn], preferred_element_type=F32)

    def finish(j, acc):
        c0 = j * tn
        if mode == "sb" and c0 < SB_Q_COLS:
            o_ref[:, c0:c0 + tn] = (acc * (-1.0 / math.sqrt(SB_HEAD_DIM))).astype(o_ref.dtype)
        elif mode == "ret" and c0 < 2 * RET_QK_COLS:
            scale = 1.0 if c0 < RET_QK_COLS else RET_QK_DIM ** -0.5
            cos = cos_ref[...] * scale
            sin = sin_ref[...] * scale
            half = RET_QK_DIM // 2
            for hh in range(tn // RET_QK_DIM):
                lo = hh * RET_QK_DIM
                x1 = acc[:, lo:lo + half]
                x2 = acc[:, lo + half:lo + 2 * half]
                o_ref[:, c0 + lo:c0 + lo + half] = (x1 * cos - x2 * sin).astype(o_ref.dtype)
                o_ref[:, c0 + lo + half:c0 + lo + 2 * half] = (x1 * sin + x2 * cos).astype(o_ref.dtype)
        else:
            o_ref[:, c0:c0 + tn] = acc.astype(o_ref.dtype)

    acc = project(0)
    for j in range(n):
        nxt = project(j + 1) if j + 1 < n else None
        finish(j, acc)
        acc = nxt


def _norm_proj(x2d, g, w, mode, rope=None, tm=512, tn=512):
    M, D = x2d.shape
    N = w.shape[1]
    in_specs = [
        pl.BlockSpec((tm, D), lambda i: (i, 0)),
        pl.BlockSpec((1, D), lambda i: (0, 0)),
        pl.BlockSpec((D, N), lambda i: (0, 0), pipeline_mode=pl.Buffered(1)),
    ]
    args = [x2d, g.reshape(1, D), w]
    if mode == "ret":
        cos, sin = rope
        nb = cos.shape[0] // tm
        in_specs += [pl.BlockSpec((tm, RET_QK_DIM // 2), lambda i: (i % nb, 0)),
                     pl.BlockSpec((tm, RET_QK_DIM // 2), lambda i: (i % nb, 0))]
        args += [cos, sin]
    return pl.pallas_call(
        functools.partial(_norm_proj_kernel, mode=mode, tn=tn),
        grid=(M // tm,),
        in_specs=in_specs,
        out_specs=pl.BlockSpec((tm, N), lambda i: (i, 0)),
        out_shape=jax.ShapeDtypeStruct((M, N), BF16),
        scratch_shapes=[pltpu.VMEM((tm, D), BF16)],
        compiler_params=_cparams(("parallel",)),
        name="norm_proj_" + mode,
    )(*args)


def _sb_attn_kernel(q_ref, k_ref, v_ref, cum_ref, o_ref, acc_ref, car_ref, *, nsub):
    T, R = SB_TILE, SB_ROWS
    qt = pl.program_id(2)
    row = lax.broadcasted_iota(jnp.int32, (R, T), 0)
    col = lax.broadcasted_iota(jnp.int32, (R, T), 1)
    head0 = col < SB_HEAD_DIM
    cum = cum_ref[...]
    nt = (((1,), (1,)), ((), ()))

    def head_queries(sub):
        q2 = q_ref[0, sub * R:(sub + 1) * R, :]
        zero = jnp.zeros_like(q2)
        return jnp.where(head0, q2, zero), jnp.where(head0, zero, q2)

    def piece(w, visible):
        lom = jnp.minimum(w, 0.0) - jnp.log(1.0 + jnp.exp2(jnp.abs(w) * (-LOG2E)))
        lsg = lom - w
        if visible is not None:
            lom = jnp.where(visible, lom, 0.0)
        hi = lom.astype(BF16)
        lo = (lom - hi.astype(F32)).astype(BF16)
        return lsg, jnp.concatenate([hi, lo], axis=1)

    def s_scores(c):
        sub, h, start, n, _ = c
        k_win = k_ref[0, pl.ds(start, n * T), :]
        return lax.dot_general(head_queries(sub)[h], k_win, nt, preferred_element_type=F32)

    def s_pieces(c, w):
        n, off = c[3], c[4]
        parts = [piece(w[:, i * T:(i + 1) * T], col < row + off if i == n - 1 else None)
                 for i in range(n)]
        return [lsg for lsg, _ in parts], jnp.concatenate([hl for _, hl in parts], axis=0)

    def s_cumsum(c, hl):
        return jnp.dot(hl, cum, preferred_element_type=F32)

    def s_weights(c, lsgs, r):
        sub, h, _, n, off = c
        carry = None
        ps = [None] * n
        for i in reversed(range(n)):
            e = lsgs[i] + r[i * R:(i + 1) * R, :T]
            if carry is not None:
                e = e + carry
            p = jnp.exp(e)
            if i == n - 1:
                p = jnp.where(col < row + off, p, 0.0)
            ps[i] = p.astype(BF16)
            tot = r[i * R:(i + 1) * R, T:]
            carry = tot if carry is None else carry + tot
        car_ref[sub, h] = carry
        return jnp.concatenate(ps, axis=1)

    def s_values(c, p):
        sub, h, start, n, _ = c
        v_win = v_ref[0, pl.ds(start, n * T), :]
        acc_ref[sub, h] = jnp.dot(p, v_win, preferred_element_type=F32)

    def near_windows(chains, group=SB_GROUP):
        groups = [chains[i:i + group] for i in range(0, len(chains), group)]
        ng = len(groups)
        w, pc, r, p = {}, {}, {}, {}
        for t in range(ng + 4):
            if t - 4 >= 0:
                for c, pv in zip(groups[t - 4], p.pop(t - 4)):
                    s_values(c, pv)
            if 0 <= t - 3 < ng:
                p[t - 3] = [s_weights(c, pcv[0], rv)
                            for c, pcv, rv in zip(groups[t - 3], pc.pop(t - 3), r.pop(t - 3))]
            if 0 <= t - 2 < ng:
                r[t - 2] = [s_cumsum(c, pcv[1]) for c, pcv in zip(groups[t - 2], pc[t - 2])]
            if 0 <= t - 1 < ng:
                pc[t - 1] = [s_pieces(c, wv) for c, wv in zip(groups[t - 1], w.pop(t - 1))]
            if t < ng:
                w[t] = [s_scores(c) for c in groups[t]]

    def first_block_chain(sub, h):
        t0 = sub * R
        if t0 >= SB_BACK:
            return (sub, h, t0 - SB_BACK, 2, T - R)
        end = -(-(t0 + R) // T) * T
        return (sub, h, 0, end // T, t0 - (end - T))

    @pl.when(qt == 0)
    def _():
        near_windows([first_block_chain(sub, h) for sub in range(nsub) for h in range(2)])

    @pl.when(qt > 0)
    def _():
        near_windows([(sub, h, pl.multiple_of(qt * (nsub * R) + sub * R - SB_BACK, R), 2, T - R)
                      for sub in range(nsub) for h in range(2)])

    def live(st):
        pos, top = st
        return jnp.logical_and(pos > 0, top > SB_LOG_ZERO)

    tops = [jnp.max(car_ref[sub]) for sub in range(nsub)]
    for sub in range(nsub):
        def far(st, sub=sub):
            pos = st[0]
            start = pl.multiple_of(jnp.maximum(pos - T, 0), R)
            fresh = col < pos - start
            k2 = k_ref[0, pl.ds(start, T), :]
            v2 = v_ref[0, pl.ds(start, T), :]
            for h, qh in enumerate(head_queries(sub)):
                w = lax.dot_general(qh, k2, nt, preferred_element_type=F32)
                lsg, hl = piece(w, fresh)
                r = jnp.dot(hl, cum, preferred_element_type=F32)
                p = jnp.where(fresh, jnp.exp(lsg + r[:, :T] + car_ref[sub, h]), 0.0)
                acc_ref[sub, h] += jnp.dot(p.astype(BF16), v2, preferred_element_type=F32)
                car_ref[sub, h] += r[:, T:]
            return start, jnp.max(car_ref[sub])

        first_pos = jnp.maximum(qt * (nsub * R) + sub * R - SB_BACK, 0)
        lax.while_loop(live, far, (first_pos, tops[sub]))

    for sub in range(nsub):
        o_ref[0, sub * R:(sub + 1) * R, :] = jnp.where(
            head0, acc_ref[sub, 0], acc_ref[sub, 1]).astype(o_ref.dtype)


def _sb_attention(qkv, B, S, tq=512):
    nsub = tq // SB_ROWS
    n_pairs = SB_Q_COLS // LANES
    T = SB_TILE
    j = jnp.arange(2 * T)[:, None] % T
    s = jnp.arange(2 * T)[None, :]
    cum = jnp.where((s >= T) | (j > s), 1.0, 0.0).astype(BF16)
    return pl.pallas_call(
        functools.partial(_sb_attn_kernel, nsub=nsub),
        grid=(B, n_pairs, S // tq),
        in_specs=[
            pl.BlockSpec((1, tq, LANES), lambda b, h, t: (b, t, h)),
            pl.BlockSpec((1, S, LANES), lambda b, h, t: (b, 0, n_pairs + h)),
            pl.BlockSpec((1, S, LANES), lambda b, h, t: (b, 0, 2 * n_pairs + h)),
            pl.BlockSpec((2 * T, 2 * T), lambda b, h, t: (0, 0)),
        ],
        out_specs=pl.BlockSpec((1, tq, LANES), lambda b, h, t: (b, t, h)),
        out_shape=jax.ShapeDtypeStruct((B, S, SB_Q_COLS), BF16),
        scratch_shapes=[
            pltpu.VMEM((nsub, 2, SB_ROWS, LANES), F32),
            pltpu.VMEM((nsub, 2, SB_ROWS, SB_TILE), F32),
        ],
        compiler_params=_cparams(("parallel", "parallel", "arbitrary")),
        name="sb_attention",
    )(qkv, qkv, qkv, cum)


def _retention_kernel(sdec_ref, q_ref, k_ref, v_ref, g_ref, gn_ref, d_ref, qdec_ref, kdec_ref,
                      o_ref, state_ref):
    t = pl.program_id(1)
    dk, dv = RET_QK_DIM, RET_V_DIM

    @pl.when(t == 0)
    def _():
        state_ref[...] = jnp.zeros_like(state_ref)

    def s_scores(h):
        q = q_ref[0, :, h * dk:(h + 1) * dk]
        k = k_ref[0, :, h * dk:(h + 1) * dk]
        return lax.dot_general(q, k, (((1,), (1,)), ((), ())), preferred_element_type=F32)

    def s_operands(h, scores):
        q = q_ref[0, :, h * dk:(h + 1) * dk]
        k = k_ref[0, :, h * dk:(h + 1) * dk]
        a = (scores * d_ref[h]).astype(BF16)
        qd = (q.astype(F32) * qdec_ref[h]).astype(BF16)
        kd = (k.astype(F32) * kdec_ref[h]).astype(BF16)
        return a, qd, kd, state_ref[h].astype(BF16)

    def s_matmuls(h, ops):
        a, qd, kd, st = ops
        v = v_ref[0, :, h * dv:(h + 1) * dv]
        o = (jnp.dot(qd, st, preferred_element_type=F32)
             + jnp.dot(a, v, preferred_element_type=F32))
        upd = lax.dot_general(kd, v, (((0,), (0,)), ((), ())), preferred_element_type=F32)
        return o, upd

    def s_finish(h, res):
        o, upd = res
        state_ref[h] = sdec_ref[h] * state_ref[h] + upd
        mu = jnp.mean(o, axis=-1, keepdims=True)
        oc = o - mu
        var = jnp.mean(oc * oc, axis=-1, keepdims=True)
        on = oc * lax.rsqrt(var + GN_EPS) * gn_ref[:, h * dv:(h + 1) * dv]
        gate = g_ref[0, :, h * dv:(h + 1) * dv].astype(F32)
        o_ref[0, :, h * dv:(h + 1) * dv] = (gate * jax.nn.sigmoid(gate) * on).astype(o_ref.dtype)

    nh = RET_HEADS
    sc, ops, res = {}, {}, {}
    for step in range(nh + 3):
        if 0 <= step - 3 < nh:
            s_finish(step - 3, res.pop(step - 3))
        if 0 <= step - 2 < nh:
            res[step - 2] = s_matmuls(step - 2, ops.pop(step - 2))
        if 0 <= step - 1 < nh:
            ops[step - 1] = s_operands(step - 1, sc.pop(step - 1))
        if step < nh:
            sc[step] = s_scores(step)


def _retention(proj, gn, consts, B, S):
    T = RET_BLOCK
    H = RET_HEADS
    qk, vw = H * RET_QK_DIM, H * RET_V_DIM
    sdec, dmat, qdec, kdec = consts
    whole = lambda b, t: (0, 0, 0)
    return pl.pallas_call(
        _retention_kernel,
        grid=(B, S // T),
        in_specs=[
            pl.BlockSpec(memory_space=pltpu.SMEM),
            pl.BlockSpec((1, T, qk), lambda b, t: (b, t, 0)),
            pl.BlockSpec((1, T, qk), lambda b, t: (b, t, 1)),
            pl.BlockSpec((1, T, vw), lambda b, t: (b, t, 2 * qk // vw)),
            pl.BlockSpec((1, T, vw), lambda b, t: (b, t, 2 * qk // vw + 1)),
            pl.BlockSpec((1, vw), lambda b, t: (0, 0)),
            pl.BlockSpec((H, T, T), whole),
            pl.BlockSpec((H, T, RET_QK_DIM), whole),
            pl.BlockSpec((H, T, RET_QK_DIM), whole),
        ],
        out_specs=pl.BlockSpec((1, T, vw), lambda b, t: (b, t, 0)),
        out_shape=jax.ShapeDtypeStruct((B, S, vw), BF16),
        scratch_shapes=[pltpu.VMEM((H, RET_QK_DIM, RET_V_DIM), F32)],
        compiler_params=_cparams(("parallel", "arbitrary")),
        name="retention",
    )(sdec, proj, proj, proj, proj, gn.reshape(1, -1), dmat, qdec, kdec)


def _retention_consts():
    T = RET_BLOCK
    log_gamma = jnp.log1p(-jnp.exp2(-RET_DECAY_BASE - jnp.arange(RET_HEADS, dtype=F32)))
    i = jnp.arange(T, dtype=F32)
    dist = jnp.abs(i[:, None] - i[None, :])
    chunk = jnp.arange(T) // CHUNK
    visible = chunk[None, :] <= chunk[:, None]
    dmat = jnp.where(visible[None], jnp.exp(log_gamma[:, None, None] * dist[None]), 0.0)
    qdec = jnp.exp(log_gamma[:, None] * (i[None] + 1.0))
    kdec = jnp.exp(log_gamma[:, None] * (T - 1.0 - i[None]))
    qdec = jnp.broadcast_to(qdec[:, :, None], (RET_HEADS, T, RET_QK_DIM))
    kdec = jnp.broadcast_to(kdec[:, :, None], (RET_HEADS, T, RET_QK_DIM))
    sdec = jnp.exp(log_gamma * T)
    return sdec.astype(F32), dmat.astype(F32), qdec.astype(F32), kdec.astype(F32)


def _rope_tables(S):
    half = RET_QK_DIM // 2
    inv_freq = ROPE_BASE ** (-jnp.arange(half, dtype=F32) / half)
    ang = jnp.arange(S, dtype=F32)[:, None] * inv_freq[None, :]
    return jnp.cos(ang), jnp.sin(ang)


def _tail_kernel(o_ref, h_ref, wo_ref, gmix_ref, gpre_ref, wup_ref, cw_ref, cb_ref, wdn_ref,
                 gpost_ref, y_ref, h1_ref, xn_ref, acc_ref, halo_ref,
                 ug0_ref, uv0_ref, ug1_ref, uv1_ref, ug2_ref, uv2_ref,
                 *, tm, cw, d_ff, tiles_per_seq):
    s = pl.program_id(0)
    n_chunks = d_ff // cw
    assert n_chunks % 2 == 1 and n_chunks >= 5
    FIRST, ODD, EVEN = (ug0_ref, uv0_ref), (ug1_ref, uv1_ref), (ug2_ref, uv2_ref)

    def col(c, base=0):
        off = c * cw + base
        return off if isinstance(off, int) else pl.multiple_of(off, LANES)

    def up(c, uslot):
        xn = xn_ref[...]
        uslot[0][...] = jnp.dot(xn, wup_ref[:, pl.ds(col(c), cw)], preferred_element_type=F32)
        uslot[1][...] = jnp.dot(xn, wup_ref[:, pl.ds(col(c, d_ff), cw)], preferred_element_type=F32)

    def conv(u, off):
        prev = halo_ref[:, pl.ds(off, cw)]
        halo_ref[:, pl.ds(off, cw)] = u[tm - 8:, :]
        ext = jnp.concatenate([prev, u], axis=0)
        w = cw_ref[:, pl.ds(off, cw)]
        return (w[2:3] * u + w[1:2] * ext[7:tm + 7] + w[0:1] * ext[6:tm + 6]
                + cb_ref[:, pl.ds(off, cw)])

    def down(c, uslot):
        gate = conv(uslot[0][...], col(c))
        val = conv(uslot[1][...], col(c, d_ff))
        act = (jax.nn.gelu(gate) * val).astype(BF16)
        acc_ref[...] += jnp.dot(act, wdn_ref[pl.ds(col(c), cw), :], preferred_element_type=F32)

    def mix_proj():
        return jnp.dot(o_ref[...], wo_ref[...], preferred_element_type=F32)

    def mix_norm(mix):
        h1 = h_ref[...] + _rms(mix, gmix_ref[...])
        h1_ref[1] = h1
        xn_ref[...] = _rms(h1, gpre_ref[...]).astype(BF16)

    @pl.when(s == 0)
    def _():
        mix_norm(mix_proj())
        up(0, FIRST)

    @pl.when(s > 0)
    def _():
        @pl.when((s - 1) % tiles_per_seq == 0)
        def _():
            halo_ref[...] = jnp.zeros_like(halo_ref)

        h1_ref[0] = h1_ref[1]
        acc_ref[...] = jnp.zeros_like(acc_ref)

        up(1, ODD)
        down(0, FIRST)
        up(2, EVEN)
        down(1, ODD)

        def pair(p, _):
            up(2 * p + 1, ODD)
            down(2 * p, EVEN)
            up(2 * p + 2, EVEN)
            down(2 * p + 1, ODD)
            return 0

        lax.fori_loop(1, (n_chunks - 3) // 2, pair, 0)
        up(n_chunks - 2, ODD)
        down(n_chunks - 3, EVEN)
        mix = mix_proj()
        up(n_chunks - 1, EVEN)
        down(n_chunks - 2, ODD)
        mix_norm(mix)
        down(n_chunks - 1, EVEN)
        up(0, FIRST)
        y_ref[...] = h1_ref[0] + _rms(acc_ref[...], gpost_ref[...])


def _layer_tail(o2d, h2d, wo, gmix, gpre, wup, conv_w, conv_b, wdn, gpost, S, tm=512, cw=256):
    M, D = h2d.shape
    Ko = o2d.shape[1]
    d_ff = wdn.shape[0]
    n_tiles = M // tm
    const = dict(pipeline_mode=pl.Buffered(1))
    ahead = lambda s: (jnp.minimum(s, n_tiles - 1), 0)
    fixed = lambda s: (0, 0)
    return pl.pallas_call(
        functools.partial(_tail_kernel, tm=tm, cw=cw, d_ff=d_ff, tiles_per_seq=S // tm),
        grid=(n_tiles + 1,),
        in_specs=[
            pl.BlockSpec((tm, Ko), ahead),
            pl.BlockSpec((tm, D), ahead),
            pl.BlockSpec((Ko, D), fixed, **const),
            pl.BlockSpec((1, D), fixed),
            pl.BlockSpec((1, D), fixed),
            pl.BlockSpec((D, 2 * d_ff), fixed, **const),
            pl.BlockSpec((CONV_WIDTH, 2 * d_ff), fixed),
            pl.BlockSpec((1, 2 * d_ff), fixed),
            pl.BlockSpec((d_ff, D), fixed, **const),
            pl.BlockSpec((1, D), fixed),
        ],
        out_specs=pl.BlockSpec((tm, D), lambda s: (jnp.maximum(s - 1, 0), 0)),
        out_shape=jax.ShapeDtypeStruct((M, D), F32),
        scratch_shapes=[
            pltpu.VMEM((2, tm, D), F32),
            pltpu.VMEM((tm, D), BF16),
            pltpu.VMEM((tm, D), F32),
            pltpu.VMEM((8, 2 * d_ff), F32),
        ] + [pltpu.VMEM((tm, cw), F32)] * 6,
        compiler_params=pltpu.CompilerParams(dimension_semantics=("arbitrary",),
                                             vmem_limit_bytes=TAIL_VMEM_LIMIT),
        name="layer_tail",
    )(o2d, h2d, wo, gmix.reshape(1, D), gpre.reshape(1, D), wup, conv_w, conv_b.reshape(1, -1),
      wdn, gpost.reshape(1, D))


def kernel(x, norm_mix_pre, norm_mix_post, sb_w_qkv, sb_w_o, ret_w_in, ret_gn, ret_w_o,
           norm_ffn_pre, norm_ffn_post, ffn_w_up, ffn_conv_w, ffn_conv_b, ffn_w_down):
    B, S, D = x.shape
    depth = norm_mix_pre.shape[0]
    h = x.reshape(B * S, D)
    rope = _rope_tables(S)
    ret_consts = _retention_consts()
    for layer in range(depth):
        j = layer // 2
        if layer % 2 == 0:
            qkv = _norm_proj(h, norm_mix_pre[layer], sb_w_qkv[j].astype(BF16), "sb")
            o = _sb_attention(qkv.reshape(B, S, -1), B, S)
            w_o = sb_w_o[j]
        else:
            proj = _norm_proj(h, norm_mix_pre[layer], ret_w_in[j].astype(BF16), "ret", rope=rope)
            o = _retention(proj.reshape(B, S, -1), ret_gn[j], ret_consts, B, S)
            w_o = ret_w_o[j]
        h = _layer_tail(o.reshape(B * S, -1), h, w_o.astype(BF16), norm_mix_post[layer],
                        norm_ffn_pre[layer], ffn_w_up[layer].astype(BF16), ffn_conv_w[layer],
                        ffn_conv_b[layer], ffn_w_down[layer].astype(BF16), norm_ffn_post[layer], S)
    return h.reshape(B, S, D)
```

```python
import functools
import math

import jax
import jax.numpy as jnp
from jax import lax
from jax.experimental import pallas as pl
from jax.experimental.pallas import tpu as pltpu

F32 = jnp.float32
BF16 = jnp.bfloat16

NORM_EPS = 1e-6
GN_EPS = 1e-5
ROPE_BASE = 10000.0
RET_DECAY_BASE = 5.0
CHUNK = 64
SB_HEAD_DIM = 64
RET_HEADS = 4
RET_QK_DIM = 256
RET_V_DIM = 512
SB_Q_COLS = 1024
RET_QK_COLS = RET_HEADS * RET_QK_DIM
CONV_WIDTH = 3

LANES = 128
SB_TILE = 128
SB_ROWS = 64
SB_BACK = 2 * SB_TILE - SB_ROWS
SB_GROUP = 8
RET_BLOCK = 256
RET_GROUP = 2
SB_LOG_ZERO = -104.0
LOG2E = 1.4426950408889634
GELU_C1 = math.sqrt(2.0 / math.pi)
GELU_C2 = GELU_C1 * 0.044715

VMEM_LIMIT = 48 * 1024 * 1024
TAIL_VMEM_LIMIT = 56 * 1024 * 1024


def _cparams(sem):
    return pltpu.CompilerParams(dimension_semantics=sem, vmem_limit_bytes=VMEM_LIMIT)


def _rms(x, g):
    ms = jnp.mean(x * x, axis=-1, keepdims=True)
    return x * lax.rsqrt(ms + NORM_EPS) * g


def _norm_proj_kernel(x_ref, g_ref, w_ref, *rest, mode, tn):
    if mode == "ret":
        cos_ref, sin_ref, o_ref, xn_ref = rest
    else:
        o_ref, xn_ref = rest
    n = o_ref.shape[1] // tn
    xn_ref[...] = _rms(x_ref[...], g_ref[...]).astype(BF16)

    def project(j):
        return jnp.dot(xn_ref[...], w_ref[:, j * tn:(j + 1) * tn], preferred_element_type=F32)

    def finish(j, acc):
        c0 = j * tn
        if mode == "sb" and c0 < SB_Q_COLS:
            o_ref[:, c0:c0 + tn] = (acc * (-1.0 / math.sqrt(SB_HEAD_DIM))).astype(o_ref.dtype)
        elif mode == "ret" and c0 < 2 * RET_QK_COLS:
            scale = 1.0 if c0 < RET_QK_COLS else RET_QK_DIM ** -0.5
            cos = cos_ref[...] * scale
            sin = sin_ref[...] * scale
            half = RET_QK_DIM // 2
            for hh in range(tn // RET_QK_DIM):
                lo = hh * RET_QK_DIM
                x1 = acc[:, lo:lo + half]
                x2 = acc[:, lo + half:lo + 2 * half]
                o_ref[:, c0 + lo:c0 + lo + half] = (x1 * cos - x2 * sin).astype(o_ref.dtype)
                o_ref[:, c0 + lo + half:c0 + lo + 2 * half] = (x1 * sin + x2 * cos).astype(o_ref.dtype)
        else:
            o_ref[:, c0:c0 + tn] = acc.astype(o_ref.dtype)

    acc = project(0)
    for j in range(n):
        nxt = project(j + 1) if j + 1 < n else None
        finish(j, acc)
        acc = nxt


def _norm_proj(x2d, g, w, mode, rope=None, tm=512, tn=512):
    M, D = x2d.shape
    N = w.shape[1]
    in_specs = [
        pl.BlockSpec((tm, D), lambda i: (i, 0)),
        pl.BlockSpec((1, D), lambda i: (0, 0)),
        pl.BlockSpec((D, N), lambda i: (0, 0), pipeline_mode=pl.Buffered(1)),
    ]
    args = [x2d, g.reshape(1, D), w]
    if mode == "ret":
        cos, sin = rope
        nb = cos.shape[0] // tm
        in_specs += [pl.BlockSpec((tm, RET_QK_DIM // 2), lambda i: (i % nb, 0)),
                     pl.BlockSpec((tm, RET_QK_DIM // 2), lambda i: (i % nb, 0))]
        args += [cos, sin]
    return pl.pallas_call(
        functools.partial(_norm_proj_kernel, mode=mode, tn=tn),
        grid=(M // tm,),
        in_specs=in_specs,
        out_specs=pl.BlockSpec((tm, N), lambda i: (i, 0)),
        out_shape=jax.ShapeDtypeStruct((M, N), BF16),
        scratch_shapes=[pltpu.VMEM((tm, D), BF16)],
        compiler_params=_cparams(("parallel",)),
        name="norm_proj_" + mode,
    )(*args)


def _sb_attn_kernel(q_ref, k_ref, v_ref, cum_ref, o_ref, acc_ref, car_ref, *, nsub):
    T, R = SB_TILE, SB_ROWS
    qt = pl.program_id(2)
    row = lax.broadcasted_iota(jnp.int32, (R, T), 0)
    col = lax.broadcasted_iota(jnp.int32, (R, T), 1)
    head0 = col < SB_HEAD_DIM
    cum = cum_ref[...]
    nt = (((1,), (1,)), ((), ()))

    def head_queries(sub):
        q2 = q_ref[0, sub * R:(sub + 1) * R, :]
        zero = jnp.zeros_like(q2)
        return jnp.where(head0, q2, zero), jnp.where(head0, zero, q2)

    def piece(w, visible):
        lom = jnp.minimum(w, 0.0) - jnp.log(1.0 + jnp.exp2(jnp.abs(w) * (-LOG2E)))
        lsg = lom - w
        if visible is not None:
            lom = jnp.where(visible, lom, 0.0)
        hi = lom.astype(BF16)
        lo = (lom - hi.astype(F32)).astype(BF16)
        return lsg, jnp.concatenate([hi, lo], axis=1)

    def s_scores(c):
        sub, h, start, n, _ = c
        k_win = k_ref[0, pl.ds(start, n * T), :]
        return lax.dot_general(head_queries(sub)[h], k_win, nt, preferred_element_type=F32)

    def s_pieces(c, w):
        n, off = c[3], c[4]
        parts = [piece(w[:, i * T:(i + 1) * T], col < row + off if i == n - 1 else None)
                 for i in range(n)]
        return [lsg for lsg, _ in parts], jnp.concatenate([hl for _, hl in parts], axis=0)

    def s_cumsum(c, hl):
        return jnp.dot(hl, cum, preferred_element_type=F32)

    def s_weights(c, lsgs, r):
        sub, h, _, n, off = c
        carry = None
        ps = [None] * n
        for i in reversed(range(n)):
            e = lsgs[i] + r[i * R:(i + 1) * R, :T]
            if carry is not None:
                e = e + carry
            p = jnp.exp(e)
            if i == n - 1:
                p = jnp.where(col < row + off, p, 0.0)
            ps[i] = p.astype(BF16)
            tot = r[i * R:(i + 1) * R, T:]
            carry = tot if carry is None else carry + tot
        car_ref[sub, h] = carry
        return jnp.concatenate(ps, axis=1)

    def s_values(c, p):
        sub, h, start, n, _ = c
        v_win = v_ref[0, pl.ds(start, n * T), :]
        acc_ref[sub, h] = jnp.dot(p, v_win, preferred_element_type=F32)

    def near_windows(chains, group=SB_GROUP):
        groups = [chains[i:i + group] for i in range(0, len(chains), group)]
        ng = len(groups)
        w, pc, r, p = {}, {}, {}, {}
        for t in range(ng + 4):
            if t - 4 >= 0:
                for c, pv in zip(groups[t - 4], p.pop(t - 4)):
                    s_values(c, pv)
            if 0 <= t - 3 < ng:
                p[t - 3] = [s_weights(c, pcv[0], rv)
                            for c, pcv, rv in zip(groups[t - 3], pc.pop(t - 3), r.pop(t - 3))]
            if 0 <= t - 2 < ng:
                r[t - 2] = [s_cumsum(c, pcv[1]) for c, pcv in zip(groups[t - 2], pc[t - 2])]
            if 0 <= t - 1 < ng:
                pc[t - 1] = [s_pieces(c, wv) for c, wv in zip(groups[t - 1], w.pop(t - 1))]
            if t < ng:
                w[t] = [s_scores(c) for c in groups[t]]

    def first_block_chain(sub, h):
        t0 = sub * R
        if t0 >= SB_BACK:
            return (sub, h, t0 - SB_BACK, 2, T - R)
        end = -(-(t0 + R) // T) * T
        return (sub, h, 0, end // T, t0 - (end - T))

    @pl.when(qt == 0)
    def _():
        near_windows([first_block_chain(sub, h) for sub in range(nsub) for h in range(2)])

    @pl.when(qt > 0)
    def _():
        near_windows([(sub, h, pl.multiple_of(qt * (nsub * R) + sub * R - SB_BACK, R), 2, T - R)
                      for sub in range(nsub) for h in range(2)])

    def live(st):
        pos, top = st
        return jnp.logical_and(pos > 0, top > SB_LOG_ZERO)

    tops = [jnp.max(car_ref[sub]) for sub in range(nsub)]
    for sub in range(nsub):
        def far(st, sub=sub):
            pos = st[0]
            start = pl.multiple_of(jnp.maximum(pos - T, 0), R)
            fresh = col < pos - start
            k2 = k_ref[0, pl.ds(start, T), :]
            v2 = v_ref[0, pl.ds(start, T), :]
            for h, qh in enumerate(head_queries(sub)):
                w = lax.dot_general(qh, k2, nt, preferred_element_type=F32)
                lsg, hl = piece(w, fresh)
                r = jnp.dot(hl, cum, preferred_element_type=F32)
                p = jnp.where(fresh, jnp.exp(lsg + r[:, :T] + car_ref[sub, h]), 0.0)
                acc_ref[sub, h] += jnp.dot(p.astype(BF16), v2, preferred_element_type=F32)
                car_ref[sub, h] += r[:, T:]
            return start, jnp.max(car_ref[sub])

        first_pos = jnp.maximum(qt * (nsub * R) + sub * R - SB_BACK, 0)
        lax.while_loop(live, far, (first_pos, tops[sub]))

    for sub in range(nsub):
        o_ref[0, sub * R:(sub + 1) * R, :] = jnp.where(
            head0, acc_ref[sub, 0], acc_ref[sub, 1]).astype(o_ref.dtype)


def _sb_attention(qkv, B, S, tq=1024):
    nsub = tq // SB_ROWS
    n_pairs = SB_Q_COLS // LANES
    T = SB_TILE
    j = jnp.arange(2 * T)[:, None] % T
    s = jnp.arange(2 * T)[None, :]
    cum = jnp.where((s >= T) | (j > s), 1.0, 0.0).astype(BF16)
    return pl.pallas_call(
        functools.partial(_sb_attn_kernel, nsub=nsub),
        grid=(B, n_pairs, S // tq),
        in_specs=[
            pl.BlockSpec((1, tq, LANES), lambda b, h, t: (b, t, h)),
            pl.BlockSpec((1, S, LANES), lambda b, h, t: (b, 0, n_pairs + h)),
            pl.BlockSpec((1, S, LANES), lambda b, h, t: (b, 0, 2 * n_pairs + h)),
            pl.BlockSpec((2 * T, 2 * T), lambda b, h, t: (0, 0)),
        ],
        out_specs=pl.BlockSpec((1, tq, LANES), lambda b, h, t: (b, t, h)),
        out_shape=jax.ShapeDtypeStruct((B, S, SB_Q_COLS), BF16),
        scratch_shapes=[
            pltpu.VMEM((nsub, 2, SB_ROWS, LANES), F32),
            pltpu.VMEM((nsub, 2, SB_ROWS, SB_TILE), F32),
        ],
        compiler_params=_cparams(("parallel", "parallel", "arbitrary")),
        name="sb_attention",
    )(qkv, qkv, qkv, cum)


def _retention_kernel(sdec_ref, q_ref, k_ref, v_ref, g_ref, gn_ref, d_ref, qdec_ref, kdec_ref,
                      o_ref, state_ref):
    t = pl.program_id(1)
    dk, dv = RET_QK_DIM, RET_V_DIM

    @pl.when(t == 0)
    def _():
        state_ref[...] = jnp.zeros_like(state_ref)

    def s_scores(h):
        q = q_ref[0, :, h * dk:(h + 1) * dk]
        k = k_ref[0, :, h * dk:(h + 1) * dk]
        return lax.dot_general(q, k, (((1,), (1,)), ((), ())), preferred_element_type=F32)

    def s_operands(h, scores):
        q = q_ref[0, :, h * dk:(h + 1) * dk]
        k = k_ref[0, :, h * dk:(h + 1) * dk]
        a = (scores * d_ref[h]).astype(BF16)
        qd = (q.astype(F32) * qdec_ref[h]).astype(BF16)
        kd = (k.astype(F32) * kdec_ref[h]).astype(BF16)
        return a, qd, kd, state_ref[h].astype(BF16)

    def s_matmuls(h, ops):
        a, qd, kd, st = ops
        v = v_ref[0, :, h * dv:(h + 1) * dv]
        o = (jnp.dot(qd, st, preferred_element_type=F32)
             + jnp.dot(a, v, preferred_element_type=F32))
        upd = lax.dot_general(kd, v, (((0,), (0,)), ((), ())), preferred_element_type=F32)
        return o, upd

    def s_finish(h, res):
        o, upd = res
        state_ref[h] = sdec_ref[h] * state_ref[h] + upd
        mu = jnp.mean(o, axis=-1, keepdims=True)
        oc = o - mu
        var = jnp.mean(oc * oc, axis=-1, keepdims=True)
        on = oc * lax.rsqrt(var + GN_EPS) * gn_ref[:, h * dv:(h + 1) * dv]
        gate = g_ref[0, :, h * dv:(h + 1) * dv].astype(F32)
        o_ref[0, :, h * dv:(h + 1) * dv] = (gate * jax.nn.sigmoid(gate) * on).astype(o_ref.dtype)

    groups = [list(range(h, h + RET_GROUP)) for h in range(0, RET_HEADS, RET_GROUP)]
    ng = len(groups)
    sc, ops, res = {}, {}, {}
    for step in range(ng + 3):
        if 0 <= step - 3 < ng:
            for h, rv in zip(groups[step - 3], res.pop(step - 3)):
                s_finish(h, rv)
        if 0 <= step - 2 < ng:
            res[step - 2] = [s_matmuls(h, ov) for h, ov in zip(groups[step - 2], ops.pop(step - 2))]
        if 0 <= step - 1 < ng:
            ops[step - 1] = [s_operands(h, sv) for h, sv in zip(groups[step - 1], sc.pop(step - 1))]
        if step < ng:
            sc[step] = [s_scores(h) for h in groups[step]]


def _retention(proj, gn, consts, B, S):
    T = RET_BLOCK
    H = RET_HEADS
    qk, vw = H * RET_QK_DIM, H * RET_V_DIM
    sdec, dmat, qdec, kdec = consts
    whole = lambda b, t: (0, 0, 0)
    return pl.pallas_call(
        _retention_kernel,
        grid=(B, S // T),
        in_specs=[
            pl.BlockSpec(memory_space=pltpu.SMEM),
            pl.BlockSpec((1, T, qk), lambda b, t: (b, t, 0)),
            pl.BlockSpec((1, T, qk), lambda b, t: (b, t, 1)),
            pl.BlockSpec((1, T, vw), lambda b, t: (b, t, 2 * qk // vw)),
            pl.BlockSpec((1, T, vw), lambda b, t: (b, t, 2 * qk // vw + 1)),
            pl.BlockSpec((1, vw), lambda b, t: (0, 0)),
            pl.BlockSpec((H, T, T), whole),
            pl.BlockSpec((H, T, RET_QK_DIM), whole),
            pl.BlockSpec((H, T, RET_QK_DIM), whole),
        ],
        out_specs=pl.BlockSpec((1, T, vw), lambda b, t: (b, t, 0)),
        out_shape=jax.ShapeDtypeStruct((B, S, vw), BF16),
        scratch_shapes=[pltpu.VMEM((H, RET_QK_DIM, RET_V_DIM), F32)],
        compiler_params=_cparams(("parallel", "arbitrary")),
        name="retention",
    )(sdec, proj, proj, proj, proj, gn.reshape(1, -1), dmat, qdec, kdec)


def _retention_consts():
    T = RET_BLOCK
    log_gamma = jnp.log1p(-jnp.exp2(-RET_DECAY_BASE - jnp.arange(RET_HEADS, dtype=F32)))
    i = jnp.arange(T, dtype=F32)
    dist = jnp.abs(i[:, None] - i[None, :])
    chunk = jnp.arange(T) // CHUNK
    visible = chunk[None, :] <= chunk[:, None]
    dmat = jnp.where(visible[None], jnp.exp(log_gamma[:, None, None] * dist[None]), 0.0)
    qdec = jnp.exp(log_gamma[:, None] * (i[None] + 1.0))
    kdec = jnp.exp(log_gamma[:, None] * (T - 1.0 - i[None]))
    qdec = jnp.broadcast_to(qdec[:, :, None], (RET_HEADS, T, RET_QK_DIM))
    kdec = jnp.broadcast_to(kdec[:, :, None], (RET_HEADS, T, RET_QK_DIM))
    sdec = jnp.exp(log_gamma * T)
    return sdec.astype(F32), dmat.astype(F32), qdec.astype(F32), kdec.astype(F32)


def _rope_tables(S):
    half = RET_QK_DIM // 2
    inv_freq = ROPE_BASE ** (-jnp.arange(half, dtype=F32) / half)
    ang = jnp.arange(S, dtype=F32)[:, None] * inv_freq[None, :]
    return jnp.cos(ang), jnp.sin(ang)


def _tail_kernel(o_ref, h_ref, wo_ref, gmix_ref, gpre_ref, wup_ref, cw_ref, cb_ref, wdn_ref,
                 gpost_ref, y_ref, h1_ref, xn_ref, acc_ref, halo_ref,
                 ug0_ref, uv0_ref, ug1_ref, uv1_ref, ug2_ref, uv2_ref,
                 *, tm, cw, d_ff, tiles_per_seq):
    s = pl.program_id(0)
    n_chunks = d_ff // cw
    assert n_chunks % 2 == 1 and n_chunks >= 5
    FIRST, ODD, EVEN = (ug0_ref, uv0_ref), (ug1_ref, uv1_ref), (ug2_ref, uv2_ref)

    def col(c, base=0):
        off = c * cw + base
        return off if isinstance(off, int) else pl.multiple_of(off, LANES)

    def up(c, uslot):
        xn = xn_ref[...]
        uslot[0][...] = jnp.dot(xn, wup_ref[:, pl.ds(col(c), cw)], preferred_element_type=F32)
        uslot[1][...] = jnp.dot(xn, wup_ref[:, pl.ds(col(c, d_ff), cw)], preferred_element_type=F32)

    def conv(u, off, scale=1.0):
        prev = halo_ref[:, pl.ds(off, cw)]
        halo_ref[:, pl.ds(off, cw)] = u[tm - 8:, :]
        ext = jnp.concatenate([prev, u], axis=0)
        w = cw_ref[:, pl.ds(off, cw)] * scale
        return (w[2:3] * u + w[1:2] * ext[7:tm + 7] + w[0:1] * ext[6:tm + 6]
                + cb_ref[:, pl.ds(off, cw)] * scale)

    def down(c, uslot):
        gate = conv(uslot[0][...], col(c))
        val = conv(uslot[1][...], col(c, d_ff), 0.5)
        t = jnp.tanh(gate * (GELU_C1 + GELU_C2 * (gate * gate)))
        gv = gate * val
        act = (gv + gv * t).astype(BF16)
        acc_ref[...] += jnp.dot(act, wdn_ref[pl.ds(col(c), cw), :], preferred_element_type=F32)

    def mix_proj():
        return jnp.dot(o_ref[...], wo_ref[...], preferred_element_type=F32)

    def mix_norm(mix):
        h1 = h_ref[...] + _rms(mix, gmix_ref[...])
        h1_ref[1] = h1
        xn_ref[...] = _rms(h1, gpre_ref[...]).astype(BF16)

    @pl.when(s == 0)
    def _():
        mix_norm(mix_proj())
        up(0, FIRST)

    @pl.when(s > 0)
    def _():
        @pl.when((s - 1) % tiles_per_seq == 0)
        def _():
            halo_ref[...] = jnp.zeros_like(halo_ref)

        h1_ref[0] = h1_ref[1]
        acc_ref[...] = jnp.zeros_like(acc_ref)

        up(1, ODD)
        down(0, FIRST)
        up(2, EVEN)
        down(1, ODD)

        def pair(p, _):
            up(2 * p + 1, ODD)
            down(2 * p, EVEN)
            up(2 * p + 2, EVEN)
            down(2 * p + 1, ODD)
            return 0

        lax.fori_loop(1, (n_chunks - 3) // 2, pair, 0)
        up(n_chunks - 2, ODD)
        down(n_chunks - 3, EVEN)
        mix = mix_proj()
        up(n_chunks - 1, EVEN)
        down(n_chunks - 2, ODD)
        mix_norm(mix)
        down(n_chunks - 1, EVEN)
        up(0, FIRST)
        y_ref[...] = h1_ref[0] + _rms(acc_ref[...], gpost_ref[...])


def _layer_tail(o2d, h2d, wo, gmix, gpre, wup, conv_w, conv_b, wdn, gpost, S, tm=512, cw=256):
    M, D = h2d.shape
    Ko = o2d.shape[1]
    d_ff = wdn.shape[0]
    n_tiles = M // tm
    const = dict(pipeline_mode=pl.Buffered(1))
    ahead = lambda s: (jnp.minimum(s, n_tiles - 1), 0)
    fixed = lambda s: (0, 0)
    return pl.pallas_call(
        functools.partial(_tail_kernel, tm=tm, cw=cw, d_ff=d_ff, tiles_per_seq=S // tm),
        grid=(n_tiles + 1,),
        in_specs=[
            pl.BlockSpec((tm, Ko), ahead),
            pl.BlockSpec((tm, D), ahead),
            pl.BlockSpec((Ko, D), fixed, **const),
            pl.BlockSpec((1, D), fixed),
            pl.BlockSpec((1, D), fixed),
            pl.BlockSpec((D, 2 * d_ff), fixed, **const),
            pl.BlockSpec((CONV_WIDTH, 2 * d_ff), fixed),
            pl.BlockSpec((1, 2 * d_ff), fixed),
            pl.BlockSpec((d_ff, D), fixed, **const),
            pl.BlockSpec((1, D), fixed),
        ],
        out_specs=pl.BlockSpec((tm, D), lambda s: (jnp.maximum(s - 1, 0), 0)),
        out_shape=jax.ShapeDtypeStruct((M, D), F32),
        scratch_shapes=[
            pltpu.VMEM((2, tm, D), F32),
            pltpu.VMEM((tm, D), BF16),
            pltpu.VMEM((tm, D), F32),
            pltpu.VMEM((8, 2 * d_ff), F32),
        ] + [pltpu.VMEM((tm, cw), F32)] * 6,
        compiler_params=pltpu.CompilerParams(dimension_semantics=("arbitrary",),
                                             vmem_limit_bytes=TAIL_VMEM_LIMIT),
        name="layer_tail",
    )(o2d, h2d, wo, gmix.reshape(1, D), gpre.reshape(1, D), wup, conv_w, conv_b.reshape(1, -1),
      wdn, gpost.reshape(1, D))


def kernel(x, norm_mix_pre, norm_mix_post, sb_w_qkv, sb_w_o, ret_w_in, ret_gn, ret_w_o,
           norm_ffn_pre, norm_ffn_post, ffn_w_up, ffn_conv_w, ffn_conv_b, ffn_w_down):
    B, S, D = x.shape
    depth = norm_mix_pre.shape[0]
    h = x.reshape(B * S, D)
    rope = _rope_tables(S)
    ret_consts = _retention_consts()
    for layer in range(depth):
        j = layer // 2
        if layer % 2 == 0:
            qkv = _norm_proj(h, norm_mix_pre[layer], sb_w_qkv[j].astype(BF16), "sb")
            o = _sb_attention(qkv.reshape(B, S, -1), B, S)
            w_o = sb_w_o[j]
        else:
            proj = _norm_proj(h, norm_mix_pre[layer], ret_w_in[j].astype(BF16), "ret", rope=rope)
            o = _retention(proj.reshape(B, S, -1), ret_gn[j], ret_consts, B, S)
            w_o = ret_w_o[j]
        h = _layer_tail(o.reshape(B * S, -1), h, w_o.astype(BF16), norm_mix_post[layer],
                        norm_ffn_pre[layer], ffn_w_up[layer].astype(BF16), ffn_conv_w[layer],
                        ffn_conv_b[layer], ffn_w_down[layer].astype(BF16), norm_ffn_post[layer], S)
    return h.reshape(B, S, D)
```

```python
import functools
import math

import jax
import jax.numpy as jnp
from jax import lax
from jax.experimental import pallas as pl
from jax.experimental.pallas import tpu as pltpu

F32 = jnp.float32
BF16 = jnp.bfloat16

NORM_EPS = 1e-6
GN_EPS = 1e-5
ROPE_BASE = 10000.0
RET_DECAY_BASE = 5.0
CHUNK = 64
SB_HEAD_DIM = 64
RET_HEADS = 4
RET_QK_DIM = 256
RET_V_DIM = 512
SB_Q_COLS = 1024
RET_QK_COLS = RET_HEADS * RET_QK_DIM
CONV_WIDTH = 3

LANES = 128
SB_TILE = 128
SB_ROWS = 64
SB_BACK = 2 * SB_TILE - SB_ROWS
SB_GROUP = 8
RET_BLOCK = 256
RET_GROUP = 2
SB_LOG_ZERO = -104.0
LOG2E = 1.4426950408889634
GELU_C1 = math.sqrt(2.0 / math.pi)
GELU_C2 = GELU_C1 * 0.044715

VMEM_LIMIT = 48 * 1024 * 1024
TAIL_VMEM_LIMIT = 56 * 1024 * 1024


def _cparams(sem):
    return pltpu.CompilerParams(dimension_semantics=sem, vmem_limit_bytes=VMEM_LIMIT)


def _rms(x, g):
    ms = jnp.mean(x * x, axis=-1, keepdims=True)
    return x * lax.rsqrt(ms + NORM_EPS) * g


def _norm_proj_kernel(x_ref, g_ref, w_ref, *rest, mode, tn):
    if mode == "ret":
        cos_ref, sin_ref, o_ref, xn_ref = rest
    else:
        o_ref, xn_ref = rest
    n = o_ref.shape[1] // tn
    xn_ref[...] = _rms(x_ref[...], g_ref[...]).astype(BF16)

    def project(j):
        return jnp.dot(xn_ref[...], w_ref[:, j * tn:(j + 1) * tn], preferred_element_type=F32)

    def finish(j, acc):
        c0 = j * tn
        if mode == "sb" and c0 < SB_Q_COLS:
            o_ref[:, c0:c0 + tn] = (acc * (-1.0 / math.sqrt(SB_HEAD_DIM))).astype(o_ref.dtype)
        elif mode == "ret" and c0 < 2 * RET_QK_COLS:
            scale = 1.0 if c0 < RET_QK_COLS else RET_QK_DIM ** -0.5
            cos = cos_ref[...] * scale
            sin = sin_ref[...] * scale
            half = RET_QK_DIM // 2
            for hh in range(tn // RET_QK_DIM):
                lo = hh * RET_QK_DIM
                x1 = acc[:, lo:lo + half]
                x2 = acc[:, lo + half:lo + 2 * half]
                o_ref[:, c0 + lo:c0 + lo + half] = (x1 * cos - x2 * sin).astype(o_ref.dtype)
                o_ref[:, c0 + lo + half:c0 + lo + 2 * half] = (x1 * sin + x2 * cos).astype(o_ref.dtype)
        else:
            o_ref[:, c0:c0 + tn] = acc.astype(o_ref.dtype)

    acc = project(0)
    for j in range(n):
        nxt = project(j + 1) if j + 1 < n else None
        finish(j, acc)
        acc = nxt


def _norm_proj(x2d, g, w, mode, rope=None, tm=512, tn=512):
    M, D = x2d.shape
    N = w.shape[1]
    in_specs = [
        pl.BlockSpec((tm, D), lambda i: (i, 0)),
        pl.BlockSpec((1, D), lambda i: (0, 0)),
        pl.BlockSpec((D, N), lambda i: (0, 0), pipeline_mode=pl.Buffered(1)),
    ]
    args = [x2d, g.reshape(1, D), w]
    if mode == "ret":
        cos, sin = rope
        nb = cos.shape[0] // tm
        in_specs += [pl.BlockSpec((tm, RET_QK_DIM // 2), lambda i: (i % nb, 0)),
                     pl.BlockSpec((tm, RET_QK_DIM // 2), lambda i: (i % nb, 0))]
        args += [cos, sin]
    return pl.pallas_call(
        functools.partial(_norm_proj_kernel, mode=mode, tn=tn),
        grid=(M // tm,),
        in_specs=in_specs,
        out_specs=pl.BlockSpec((tm, N), lambda i: (i, 0)),
        out_shape=jax.ShapeDtypeStruct((M, N), BF16),
        scratch_shapes=[pltpu.VMEM((tm, D), BF16)],
        compiler_params=_cparams(("parallel",)),
        name="norm_proj_" + mode,
    )(*args)


def _sb_attn_kernel(q_ref, k_ref, v_ref, cum_ref, o_ref, acc_ref, car_ref, *, nsub):
    T, R = SB_TILE, SB_ROWS
    qt = pl.program_id(2)
    row = lax.broadcasted_iota(jnp.int32, (R, T), 0)
    col = lax.broadcasted_iota(jnp.int32, (R, T), 1)
    head0 = col < SB_HEAD_DIM
    cum = cum_ref[...]
    nt = (((1,), (1,)), ((), ()))

    def head_queries(sub):
        q2 = q_ref[0, sub * R:(sub + 1) * R, :]
        zero = jnp.zeros_like(q2)
        return jnp.where(head0, q2, zero), jnp.where(head0, zero, q2)

    def piece(w, visible):
        lom = jnp.minimum(w, 0.0) - jnp.log(1.0 + jnp.exp2(jnp.abs(w) * (-LOG2E)))
        lsg = lom - w
        if visible is not None:
            lom = jnp.where(visible, lom, 0.0)
        hi = lom.astype(BF16)
        lo = (lom - hi.astype(F32)).astype(BF16)
        return lsg, jnp.concatenate([hi, lo], axis=1)

    def s_scores(c):
        sub, h, start, n, _ = c
        k_win = k_ref[0, pl.ds(start, n * T), :]
        return lax.dot_general(head_queries(sub)[h], k_win, nt, preferred_element_type=F32)

    def s_pieces(c, w):
        n, off = c[3], c[4]
        parts = [piece(w[:, i * T:(i + 1) * T], col < row + off if i == n - 1 else None)
                 for i in range(n)]
        return [lsg for lsg, _ in parts], jnp.concatenate([hl for _, hl in parts], axis=0)

    def s_cumsum(c, hl):
        return jnp.dot(hl, cum, preferred_element_type=F32)

    def s_weights(c, lsgs, r):
        sub, h, _, n, off = c
        carry = None
        ps = [None] * n
        for i in reversed(range(n)):
            e = lsgs[i] + r[i * R:(i + 1) * R, :T]
            if carry is not None:
                e = e + carry
            p = jnp.exp(e)
            if i == n - 1:
                p = jnp.where(col < row + off, p, 0.0)
            ps[i] = p.astype(BF16)
            tot = r[i * R:(i + 1) * R, T:]
            carry = tot if carry is None else carry + tot
        car_ref[sub, h] = carry
        return jnp.concatenate(ps, axis=1)

    def s_values(c, p):
        sub, h, start, n, _ = c
        v_win = v_ref[0, pl.ds(start, n * T), :]
        acc_ref[sub, h] = jnp.dot(p, v_win, preferred_element_type=F32)

    def near_windows(chains, group=SB_GROUP):
        groups = [chains[i:i + group] for i in range(0, len(chains), group)]
        ng = len(groups)
        w, pc, r, p = {}, {}, {}, {}
        for t in range(ng + 4):
            if t - 4 >= 0:
                for c, pv in zip(groups[t - 4], p.pop(t - 4)):
                    s_values(c, pv)
            if 0 <= t - 3 < ng:
                p[t - 3] = [s_weights(c, pcv[0], rv)
                            for c, pcv, rv in zip(groups[t - 3], pc.pop(t - 3), r.pop(t - 3))]
            if 0 <= t - 2 < ng:
                r[t - 2] = [s_cumsum(c, pcv[1]) for c, pcv in zip(groups[t - 2], pc[t - 2])]
            if 0 <= t - 1 < ng:
                pc[t - 1] = [s_pieces(c, wv) for c, wv in zip(groups[t - 1], w.pop(t - 1))]
            if t < ng:
                w[t] = [s_scores(c) for c in groups[t]]

    def first_block_chain(sub, h):
        t0 = sub * R
        if t0 >= SB_BACK:
            return (sub, h, t0 - SB_BACK, 2, T - R)
        end = -(-(t0 + R) // T) * T
        return (sub, h, 0, end // T, t0 - (end - T))

    @pl.when(qt == 0)
    def _():
        near_windows([first_block_chain(sub, h) for sub in range(nsub) for h in range(2)])

    @pl.when(qt > 0)
    def _():
        near_windows([(sub, h, pl.multiple_of(qt * (nsub * R) + sub * R - SB_BACK, R), 2, T - R)
                      for sub in range(nsub) for h in range(2)])

    def live(st):
        pos, top = st
        return jnp.logical_and(pos > 0, top > SB_LOG_ZERO)

    tops = [jnp.max(car_ref[sub]) for sub in range(nsub)]
    for sub in range(nsub):
        def far(st, sub=sub):
            pos = st[0]
            start = pl.multiple_of(jnp.maximum(pos - T, 0), R)
            fresh = col < pos - start
            k2 = k_ref[0, pl.ds(start, T), :]
            v2 = v_ref[0, pl.ds(start, T), :]
            for h, qh in enumerate(head_queries(sub)):
                w = lax.dot_general(qh, k2, nt, preferred_element_type=F32)
                lsg, hl = piece(w, fresh)
                r = jnp.dot(hl, cum, preferred_element_type=F32)
                p = jnp.where(fresh, jnp.exp(lsg + r[:, :T] + car_ref[sub, h]), 0.0)
                acc_ref[sub, h] += jnp.dot(p.astype(BF16), v2, preferred_element_type=F32)
                car_ref[sub, h] += r[:, T:]
            return start, jnp.max(car_ref[sub])

        first_pos = jnp.maximum(qt * (nsub * R) + sub * R - SB_BACK, 0)
        lax.while_loop(live, far, (first_pos, tops[sub]))

    for sub in range(nsub):
        o_ref[0, sub * R:(sub + 1) * R, :] = jnp.where(
            head0, acc_ref[sub, 0], acc_ref[sub, 1]).astype(o_ref.dtype)


def _sb_attention(qkv, B, S, tq=2048):
    nsub = tq // SB_ROWS
    n_pairs = SB_Q_COLS // LANES
    T = SB_TILE
    j = jnp.arange(2 * T)[:, None] % T
    s = jnp.arange(2 * T)[None, :]
    cum = jnp.where((s >= T) | (j > s), 1.0, 0.0).astype(BF16)
    return pl.pallas_call(
        functools.partial(_sb_attn_kernel, nsub=nsub),
        grid=(B, n_pairs, S // tq),
        in_specs=[
            pl.BlockSpec((1, tq, LANES), lambda b, h, t: (b, t, h)),
            pl.BlockSpec((1, S, LANES), lambda b, h, t: (b, 0, n_pairs + h)),
            pl.BlockSpec((1, S, LANES), lambda b, h, t: (b, 0, 2 * n_pairs + h)),
            pl.BlockSpec((2 * T, 2 * T), lambda b, h, t: (0, 0)),
        ],
        out_specs=pl.BlockSpec((1, tq, LANES), lambda b, h, t: (b, t, h)),
        out_shape=jax.ShapeDtypeStruct((B, S, SB_Q_COLS), BF16),
        scratch_shapes=[
            pltpu.VMEM((nsub, 2, SB_ROWS, LANES), F32),
            pltpu.VMEM((nsub, 2, SB_ROWS, SB_TILE), F32),
        ],
        compiler_params=_cparams(("parallel", "parallel", "arbitrary")),
        name="sb_attention",
    )(qkv, qkv, qkv, cum)


def _retention_kernel(sdec_ref, q_ref, k_ref, v_ref, g_ref, gn_ref, d_ref, qdec_ref, kdec_ref,
                      o_ref, state_ref):
    t = pl.program_id(1)
    dk, dv = RET_QK_DIM, RET_V_DIM

    @pl.when(t == 0)
    def _():
        state_ref[...] = jnp.zeros_like(state_ref)

    def s_scores(h):
        q = q_ref[0, :, h * dk:(h + 1) * dk]
        k = k_ref[0, :, h * dk:(h + 1) * dk]
        return lax.dot_general(q, k, (((1,), (1,)), ((), ())), preferred_element_type=F32)

    def s_operands(h, scores):
        q = q_ref[0, :, h * dk:(h + 1) * dk]
        k = k_ref[0, :, h * dk:(h + 1) * dk]
        a = (scores * d_ref[h]).astype(BF16)
        qd = (q.astype(F32) * qdec_ref[h]).astype(BF16)
        kd = (k.astype(F32) * kdec_ref[h]).astype(BF16)
        return a, qd, kd, state_ref[h].astype(BF16)

    def s_matmuls(h, ops):
        a, qd, kd, st = ops
        v = v_ref[0, :, h * dv:(h + 1) * dv]
        o = (jnp.dot(qd, st, preferred_element_type=F32)
             + jnp.dot(a, v, preferred_element_type=F32))
        upd = lax.dot_general(kd, v, (((0,), (0,)), ((), ())), preferred_element_type=F32)
        return o, upd

    def s_finish(h, res):
        o, upd = res
        state_ref[h] = sdec_ref[h] * state_ref[h] + upd
        mu = jnp.mean(o, axis=-1, keepdims=True)
        oc = o - mu
        var = jnp.mean(oc * oc, axis=-1, keepdims=True)
        on = oc * lax.rsqrt(var + GN_EPS) * gn_ref[:, h * dv:(h + 1) * dv]
        gate = g_ref[0, :, h * dv:(h + 1) * dv].astype(F32)
        o_ref[0, :, h * dv:(h + 1) * dv] = (gate * jax.nn.sigmoid(gate) * on).astype(o_ref.dtype)

    groups = [list(range(h, h + RET_GROUP)) for h in range(0, RET_HEADS, RET_GROUP)]
    ng = len(groups)
    sc, ops, res = {}, {}, {}
    for step in range(ng + 3):
        if 0 <= step - 3 < ng:
            for h, rv in zip(groups[step - 3], res.pop(step - 3)):
                s_finish(h, rv)
        if 0 <= step - 2 < ng:
            res[step - 2] = [s_matmuls(h, ov) for h, ov in zip(groups[step - 2], ops.pop(step - 2))]
        if 0 <= step - 1 < ng:
            ops[step - 1] = [s_operands(h, sv) for h, sv in zip(groups[step - 1], sc.pop(step - 1))]
        if step < ng:
            sc[step] = [s_scores(h) for h in groups[step]]


def _retention(proj, gn, consts, B, S):
    T = RET_BLOCK
    H = RET_HEADS
    qk, vw = H * RET_QK_DIM, H * RET_V_DIM
    sdec, dmat, qdec, kdec = consts
    whole = lambda b, t: (0, 0, 0)
    return pl.pallas_call(
        _retention_kernel,
        grid=(B, S // T),
        in_specs=[
            pl.BlockSpec(memory_space=pltpu.SMEM),
            pl.BlockSpec((1, T, qk), lambda b, t: (b, t, 0)),
            pl.BlockSpec((1, T, qk), lambda b, t: (b, t, 1)),
            pl.BlockSpec((1, T, vw), lambda b, t: (b, t, 2 * qk // vw)),
            pl.BlockSpec((1, T, vw), lambda b, t: (b, t, 2 * qk // vw + 1)),
            pl.BlockSpec((1, vw), lambda b, t: (0, 0)),
            pl.BlockSpec((H, T, T), whole),
            pl.BlockSpec((H, T, RET_QK_DIM), whole),
            pl.BlockSpec((H, T, RET_QK_DIM), whole),
        ],
        out_specs=pl.BlockSpec((1, T, vw), lambda b, t: (b, t, 0)),
        out_shape=jax.ShapeDtypeStruct((B, S, vw), BF16),
        scratch_shapes=[pltpu.VMEM((H, RET_QK_DIM, RET_V_DIM), F32)],
        compiler_params=_cparams(("parallel", "arbitrary")),
        name="retention",
    )(sdec, proj, proj, proj, proj, gn.reshape(1, -1), dmat, qdec, kdec)


def _retention_consts():
    T = RET_BLOCK
    log_gamma = jnp.log1p(-jnp.exp2(-RET_DECAY_BASE - jnp.arange(RET_HEADS, dtype=F32)))
    i = jnp.arange(T, dtype=F32)
    dist = jnp.abs(i[:, None] - i[None, :])
    chunk = jnp.arange(T) // CHUNK
    visible = chunk[None, :] <= chunk[:, None]
    dmat = jnp.where(visible[None], jnp.exp(log_gamma[:, None, None] * dist[None]), 0.0)
    qdec = jnp.exp(log_gamma[:, None] * (i[None] + 1.0))
    kdec = jnp.exp(log_gamma[:, None] * (T - 1.0 - i[None]))
    qdec = jnp.broadcast_to(qdec[:, :, None], (RET_HEADS, T, RET_QK_DIM))
    kdec = jnp.broadcast_to(kdec[:, :, None], (RET_HEADS, T, RET_QK_DIM))
    sdec = jnp.exp(log_gamma * T)
    return sdec.astype(F32), dmat.astype(F32), qdec.astype(F32), kdec.astype(F32)


def _rope_tables(S):
    half = RET_QK_DIM // 2
    inv_freq = ROPE_BASE ** (-jnp.arange(half, dtype=F32) / half)
    ang = jnp.arange(S, dtype=F32)[:, None] * inv_freq[None, :]
    return jnp.cos(ang), jnp.sin(ang)


def _tail_kernel(o_ref, h_ref, wo_ref, gmix_ref, gpre_ref, wup_ref, cw_ref, cb_ref, wdn_ref,
                 gpost_ref, y_ref, h1_ref, xn_ref, acc_ref, halo_ref,
                 ug0_ref, uv0_ref, ug1_ref, uv1_ref, ug2_ref, uv2_ref,
                 *, tm, cw, d_ff, tiles_per_seq):
    s = pl.program_id(0)
    n_chunks = d_ff // cw
    assert n_chunks % 2 == 1 and n_chunks >= 5
    FIRST, ODD, EVEN = (ug0_ref, uv0_ref), (ug1_ref, uv1_ref), (ug2_ref, uv2_ref)

    def col(c, base=0):
        off = c * cw + base
        return off if isinstance(off, int) else pl.multiple_of(off, LANES)

    def up(c, uslot):
        xn = xn_ref[...]
        uslot[0][...] = jnp.dot(xn, wup_ref[:, pl.ds(col(c), cw)], preferred_element_type=F32)
        uslot[1][...] = jnp.dot(xn, wup_ref[:, pl.ds(col(c, d_ff), cw)], preferred_element_type=F32)

    def conv(u, off, scale=1.0):
        prev = halo_ref[:, pl.ds(off, cw)]
        halo_ref[:, pl.ds(off, cw)] = u[tm - 8:, :]
        ext = jnp.concatenate([prev, u], axis=0)
        w = cw_ref[:, pl.ds(off, cw)] * scale
        return (w[2:3] * u + w[1:2] * ext[7:tm + 7] + w[0:1] * ext[6:tm + 6]
                + cb_ref[:, pl.ds(off, cw)] * scale)

    def down(c, uslot):
        gate = conv(uslot[0][...], col(c))
        val = conv(uslot[1][...], col(c, d_ff), 0.5)
        t = jnp.tanh(gate * (GELU_C1 + GELU_C2 * (gate * gate)))
        gv = gate * val
        act = (gv + gv * t).astype(BF16)
        acc_ref[...] += jnp.dot(act, wdn_ref[pl.ds(col(c), cw), :], preferred_element_type=F32)

    def mix_proj():
        return jnp.dot(o_ref[...], wo_ref[...], preferred_element_type=F32)

    def mix_norm(mix):
        h1 = h_ref[...] + _rms(mix, gmix_ref[...])
        h1_ref[1] = h1
        xn_ref[...] = _rms(h1, gpre_ref[...]).astype(BF16)

    @pl.when(s == 0)
    def _():
        mix_norm(mix_proj())
        up(0, FIRST)

    @pl.when(s > 0)
    def _():
        @pl.when((s - 1) % tiles_per_seq == 0)
        def _():
            halo_ref[...] = jnp.zeros_like(halo_ref)

        h1_ref[0] = h1_ref[1]
        acc_ref[...] = jnp.zeros_like(acc_ref)

        up(1, ODD)
        down(0, FIRST)
        up(2, EVEN)
        down(1, ODD)

        def pair(p, _):
            up(2 * p + 1, ODD)
            down(2 * p, EVEN)
            up(2 * p + 2, EVEN)
            down(2 * p + 1, ODD)
            return 0

        lax.fori_loop(1, (n_chunks - 3) // 2, pair, 0)
        up(n_chunks - 2, ODD)
        down(n_chunks - 3, EVEN)
        mix = mix_proj()
        up(n_chunks - 1, EVEN)
        down(n_chunks - 2, ODD)
        mix_norm(mix)
        down(n_chunks - 1, EVEN)
        up(0, FIRST)
        y_ref[...] = h1_ref[0] + _rms(acc_ref[...], gpost_ref[...])


def _layer_tail(o2d, h2d, wo, gmix, gpre, wup, conv_w, conv_b, wdn, gpost, S, tm=512, cw=256):
    M, D = h2d.shape
    Ko = o2d.shape[1]
    d_ff = wdn.shape[0]
    n_tiles = M // tm
    const = dict(pipeline_mode=pl.Buffered(1))
    ahead = lambda s: (jnp.minimum(s, n_tiles - 1), 0)
    fixed = lambda s: (0, 0)
    return pl.pallas_call(
        functools.partial(_tail_kernel, tm=tm, cw=cw, d_ff=d_ff, tiles_per_seq=S // tm),
        grid=(n_tiles + 1,),
        in_specs=[
            pl.BlockSpec((tm, Ko), ahead),
            pl.BlockSpec((tm, D), ahead),
            pl.BlockSpec((Ko, D), fixed, **const),
            pl.BlockSpec((1, D), fixed),
            pl.BlockSpec((1, D), fixed),
            pl.BlockSpec((D, 2 * d_ff), fixed, **const),
            pl.BlockSpec((CONV_WIDTH, 2 * d_ff), fixed),
            pl.BlockSpec((1, 2 * d_ff), fixed),
            pl.BlockSpec((d_ff, D), fixed, **const),
            pl.BlockSpec((1, D), fixed),
        ],
        out_specs=pl.BlockSpec((tm, D), lambda s: (jnp.maximum(s - 1, 0), 0)),
        out_shape=jax.ShapeDtypeStruct((M, D), F32),
        scratch_shapes=[
            pltpu.VMEM((2, tm, D), F32),
            pltpu.VMEM((tm, D), BF16),
            pltpu.VMEM((tm, D), F32),
            pltpu.VMEM((8, 2 * d_ff), F32),
        ] + [pltpu.VMEM((tm, cw), F32)] * 6,
        compiler_params=pltpu.CompilerParams(dimension_semantics=("arbitrary",),
                                             vmem_limit_bytes=TAIL_VMEM_LIMIT),
        name="layer_tail",
    )(o2d, h2d, wo, gmix.reshape(1, D), gpre.reshape(1, D), wup, conv_w, conv_b.reshape(1, -1),
      wdn, gpost.reshape(1, D))


def kernel(x, norm_mix_pre, norm_mix_post, sb_w_qkv, sb_w_o, ret_w_in, ret_gn, ret_w_o,
           norm_ffn_pre, norm_ffn_post, ffn_w_up, ffn_conv_w, ffn_conv_b, ffn_w_down):
    B, S, D = x.shape
    depth = norm_mix_pre.shape[0]
    h = x.reshape(B * S, D)
    rope = _rope_tables(S)
    ret_consts = _retention_consts()
    for layer in range(depth):
        j = layer // 2
        if layer % 2 == 0:
            qkv = _norm_proj(h, norm_mix_pre[layer], sb_w_qkv[j].astype(BF16), "sb")
            o = _sb_attention(qkv.reshape(B, S, -1), B, S)
            w_o = sb_w_o[j]
        else:
            proj = _norm_proj(h, norm_mix_pre[layer], ret_w_in[j].astype(BF16), "ret", rope=rope)
            o = _retention(proj.reshape(B, S, -1), ret_gn[j], ret_consts, B, S)
            w_o = ret_w_o[j]
        h = _layer_tail(o.reshape(B * S, -1), h, w_o.astype(BF16), norm_mix_post[layer],
                        norm_ffn_pre[layer], ffn_w_up[layer].astype(BF16), ffn_conv_w[layer],
                        ffn_conv_b[layer], ffn_w_down[layer].astype(BF16), norm_ffn_post[layer], S)
    return h.reshape(B, S, D)
```
